```python
import jax, jax.numpy as jnp
from jax import lax
import numpy as np

D_MODEL = 1024
BATCH = 8
SEQ = 8192
DEPTH = 1
DEC_BATCH = 128
DEC_SEQ = 4
PAST_LEN = 8192
PAGE_SIZE = 128

D_MIX = D_MODEL
HEAD_DIM = 64
N_HEADS_A = (D_MIX // 2) // HEAD_DIM
D_A = N_HEADS_A * HEAD_DIM
D_B = D_MIX - D_A
CHUNK = 128
GROUP_B = 128
N_GROUPS_B = D_B // GROUP_B
D_FF = 2816
WINDOWS = (128, 512, 2048)
DILATIONS = (1, 4, 16)
W_MAX = max(WINDOWS)
ATTN_SCALE = HEAD_DIM ** -0.5
LN_EPS = 1e-5
NEG_INF = -1e30
DEEPNORM_ALPHA = (2.0 * DEPTH) ** 0.25
DEEPNORM_BETA = (8.0 * DEPTH) ** -0.25
FFN_HALF = 0.5

kernel_name = "hymba_longnet_gmlp_macaron_deepnorm_step"


def _layer_norm(x, g, b):
    xf = x.astype(jnp.float32)
    mu = xf.mean(-1, keepdims=True)
    var = jnp.square(xf - mu).mean(-1, keepdims=True)
    return ((xf - mu) * lax.rsqrt(var + LN_EPS) * g + b).astype(x.dtype)


def _rms_norm(x, g):
    xf = x.astype(jnp.float32)
    return (xf * lax.rsqrt(jnp.mean(xf * xf, -1, keepdims=True) + LN_EPS) * g).astype(x.dtype)


def _swiglu(x, w_in, w_out):
    gate, up = jnp.split(x @ w_in, 2, axis=-1)
    return (jax.nn.silu(gate) * up) @ w_out


def _attend_band(q, k, v, dilation, n_sub):
    B, S, H, Dh = q.shape
    span = dilation * n_sub
    L = -(-S // span) * span
    nb = L // span

    def blocks(t):
        t = jnp.pad(t.astype(jnp.float32), ((0, 0), (0, L - S), (0, 0), (0, 0)))
        return t.reshape(B, nb, n_sub, dilation, H, Dh)

    def with_prev(t):
        prev = jnp.concatenate([jnp.zeros_like(t[:, :1]), t[:, :-1]], axis=1)
        return jnp.concatenate([prev, t], axis=2)

    qb = blocks(q)
    kk = with_prev(blocks(k))
    vv = with_prev(blocks(v))
    s = jnp.einsum('bnirhd,bnjrhd->bnrhij', qb, kk) * ATTN_SCALE
    i = jnp.arange(n_sub)[:, None]
    j = jnp.arange(2 * n_sub)[None, :]
    diff = i + n_sub - j
    band = (diff >= 0) & (diff <= n_sub)
    exists = (jnp.arange(nb)[:, None, None] > 0) | (j[None] >= n_sub)
    mask = band[None] & exists
    s = jnp.where(mask[None, :, None, None], s, NEG_INF)
    m = s.max(-1, keepdims=True)
    e = jnp.exp(s - m)
    denom = e.sum(-1)
    o = jnp.einsum('bnrhij,bnjrhd->bnirhd', e, vv) / denom.transpose(0, 1, 4, 2, 3)[..., None]
    lse = (m[..., 0] + jnp.log(denom)).transpose(0, 1, 4, 2, 3)
    return o.reshape(B, L, H, Dh)[:, :S], lse.reshape(B, L, H)[:, :S]


def _attend_gathered(q, k_all, v_all, dilation, n_sub):
    T = q.shape[1]
    w_buf = k_all.shape[1] - T
    dist = jnp.arange(n_sub + 1) * dilation
    idx = w_buf + jnp.arange(T)[:, None] - dist[None, :]
    valid = idx >= 0
    idx = jnp.maximum(idx, 0)
    kg = jnp.take(k_all, idx, axis=1).astype(jnp.float32)
    vg = jnp.take(v_all, idx, axis=1).astype(jnp.float32)
    s = jnp.einsum('bthd,btjhd->bthj', q.astype(jnp.float32), kg) * ATTN_SCALE
    s = jnp.where(valid[None, :, None, :], s, NEG_INF)
    m = s.max(-1, keepdims=True)
    e = jnp.exp(s - m)
    denom = e.sum(-1)
    o = jnp.einsum('bthj,btjhd->bthd', e, vg) / denom[..., None]
    return o, m[..., 0] + jnp.log(denom)


def _combine_by_denominator(outs, lses):
    w = jax.nn.softmax(jnp.stack(lses), axis=0)
    return jnp.sum(w[..., None] * jnp.stack(outs), axis=0)


def _spatial_gating(u, v, w_s, b_s, g_v, b_v):
    B, S, _ = v.shape
    vn = _layer_norm(v, g_v, b_v)
    L = -(-S // CHUNK) * CHUNK
    vc = jnp.pad(vn, ((0, 0), (0, L - S), (0, 0))).reshape(B, L // CHUNK, CHUNK, N_GROUPS_B, GROUP_B)
    causal = jnp.tril(jnp.ones((CHUNK, CHUNK), dtype=bool))
    w = jnp.where(causal[None], w_s, 0.0)
    mix = jnp.einsum('gij,bnjgc->bnigc', w, vc) + b_s.T[None, None, :, :, None]
    return u * mix.reshape(B, L, D_B)[:, :S], vn


def _layer(x, p, cache_k, cache_v):
    (ffn1_w_in, ffn1_w_out, ln1_g, ln1_b, w_in, sgu_w, sgu_b, sgu_v_g, sgu_v_b,
     out_a_g, out_b_g, w_out, ln2_g, ln2_b, ffn2_w_in, ffn2_w_out, ln3_g, ln3_b) = p
    Bx, S, _ = x.shape
    x = _layer_norm(DEEPNORM_ALPHA * x + FFN_HALF * _swiglu(x, ffn1_w_in, ffn1_w_out), ln1_g, ln1_b)
    h = x @ w_in
    q, k, v, u_b, v_b = jnp.split(h, [D_A, 2 * D_A, 3 * D_A, 3 * D_A + D_B], axis=-1)
    q = q.reshape(Bx, S, N_HEADS_A, HEAD_DIM)
    k = k.reshape(Bx, S, N_HEADS_A, HEAD_DIM)
    v = v.reshape(Bx, S, N_HEADS_A, HEAD_DIM)
    outs, lses = [], []
    if cache_k is None:
        for win, dil in zip(WINDOWS, DILATIONS):
            o, l = _attend_band(q, k, v, dil, win // dil)
            outs.append(o)
            lses.append(l)
        keep = min(W_MAX, S)
        k_state, v_state = k[:, S - keep:], v[:, S - keep:]
    else:
        k_all = jnp.concatenate([cache_k.astype(k.dtype), k], axis=1)
        v_all = jnp.concatenate([cache_v.astype(v.dtype), v], axis=1)
        for win, dil in zip(WINDOWS, DILATIONS):
            o, l = _attend_gathered(q, k_all, v_all, dil, win // dil)
            outs.append(o)
            lses.append(l)
        k_state, v_state = k_all[:, S:], v_all[:, S:]
    o_a = _combine_by_denominator(outs, lses).astype(x.dtype).reshape(Bx, S, D_A)
    o_b, vn = _spatial_gating(u_b, v_b, sgu_w, sgu_b, sgu_v_g, sgu_v_b)
    mixed = jnp.concatenate([_rms_norm(o_a, out_a_g), _rms_norm(o_b, out_b_g)], axis=-1) @ w_out
    x = _layer_norm(DEEPNORM_ALPHA * x + mixed, ln2_g, ln2_b)
    x = _layer_norm(DEEPNORM_ALPHA * x + FFN_HALF * _swiglu(x, ffn2_w_in, ffn2_w_out), ln3_g, ln3_b)
    return x, k_state, v_state, vn


def setup_inputs(seed: int = 0) -> dict:
    key = jax.random.key(seed)
    ks = jax.random.split(key, 24)
    f32 = jnp.float32
    w_buf = min(W_MAX, PAST_LEN)

    def nrm(k, shape, scale=1.0):
        return jax.random.normal(k, shape, f32) * scale

    return {
        "x_prompt": nrm(ks[0], (BATCH, SEQ, D_MODEL)),
        "x_sample": nrm(ks[1], (DEC_BATCH, DEC_SEQ, D_MODEL)),
        "cache_k": nrm(ks[2], (DEPTH, DEC_BATCH, w_buf, N_HEADS_A, HEAD_DIM)),
        "cache_v": nrm(ks[3], (DEPTH, DEC_BATCH, w_buf, N_HEADS_A, HEAD_DIM)),
        "ffn1_w_in": nrm(ks[4], (DEPTH, D_MODEL, 2 * D_FF), D_MODEL ** -0.5),
        "ffn1_w_out": nrm(ks[5], (DEPTH, D_FF, D_MODEL), DEEPNORM_BETA * D_FF ** -0.5),
        "ln1_g": 1.0 + nrm(ks[6], (DEPTH, D_MODEL), 0.01),
        "ln1_b": nrm(ks[7], (DEPTH, D_MODEL), 0.01),
        "w_in": nrm(ks[8], (DEPTH, D_MODEL, 3 * D_A + 2 * D_B), D_MODEL ** -0.5),
        "sgu_w": nrm(ks[9], (DEPTH, N_GROUPS_B, CHUNK, CHUNK), CHUNK ** -0.5),
        "sgu_b": 1.0 + nrm(ks[10], (DEPTH, N_GROUPS_B, CHUNK), 0.01),
        "sgu_v_g": 1.0 + nrm(ks[11], (DEPTH, D_B), 0.01),
        "sgu_v_b": nrm(ks[12], (DEPTH, D_B), 0.01),
        "out_a_g": 1.0 + nrm(ks[13], (DEPTH, D_A), 0.01),
        "out_b_g": 1.0 + nrm(ks[14], (DEPTH, D_B), 0.01),
        "w_out": nrm(ks[15], (DEPTH, D_MIX, D_MODEL), DEEPNORM_BETA * D_MIX ** -0.5),
        "ln2_g": 1.0 + nrm(ks[16], (DEPTH, D_MODEL), 0.01),
        "ln2_b": nrm(ks[17], (DEPTH, D_MODEL), 0.01),
        "ffn2_w_in": nrm(ks[18], (DEPTH, D_MODEL, 2 * D_FF), D_MODEL ** -0.5),
        "ffn2_w_out": nrm(ks[19], (DEPTH, D_FF, D_MODEL), DEEPNORM_BETA * D_FF ** -0.5),
        "ln3_g": 1.0 + nrm(ks[20], (DEPTH, D_MODEL), 0.01),
        "ln3_b": nrm(ks[21], (DEPTH, D_MODEL), 0.01),
    }


def reference(x_prompt, x_sample, cache_k, cache_v, ffn1_w_in, ffn1_w_out, ln1_g, ln1_b, w_in,
              sgu_w, sgu_b, sgu_v_g, sgu_v_b, out_a_g, out_b_g, w_out, ln2_g, ln2_b,
              ffn2_w_in, ffn2_w_out, ln3_g, ln3_b):
    params = (ffn1_w_in, ffn1_w_out, ln1_g, ln1_b, w_in, sgu_w, sgu_b, sgu_v_g, sgu_v_b,
              out_a_g, out_b_g, w_out, ln2_g, ln2_b, ffn2_w_in, ffn2_w_out, ln3_g, ln3_b)
    y_p, y_s = x_prompt, x_sample
    kp_list, vp_list, ks_list, vs_list, us_list = [], [], [], [], []
    for layer in range(DEPTH):
        p = tuple(a[layer] for a in params)
        y_p, kp, vp, _ = _layer(y_p, p, None, None)
        y_s, ksmp, vsmp, usmp = _layer(y_s, p, cache_k[layer], cache_v[layer])
        kp_list.append(kp)
        vp_list.append(vp)
        ks_list.append(ksmp)
        vs_list.append(vsmp)
        us_list.append(usmp)
    cache_k_prompt = jnp.stack(kp_list)
    cache_v_prompt = jnp.stack(vp_list)
    cache_k_sample = jnp.stack(ks_list)
    cache_v_sample = jnp.stack(vs_list)
    sgu_v_sample = jnp.stack(us_list)
    return (y_p, y_s, cache_k_prompt, cache_v_prompt, cache_k_sample, cache_v_sample, sgu_v_sample)
```

```python
import functools
import math

import numpy as np
import jax
import jax.numpy as jnp
from jax import lax
from jax.experimental import pallas as pl
from jax.experimental.pallas import tpu as pltpu

F32 = jnp.float32
BF16 = jnp.bfloat16

D_MODEL = 1024
HEAD_DIM = 64
N_HEADS = 8
D_A = N_HEADS * HEAD_DIM
D_B = D_MODEL - D_A
D_FF = 2816
CHUNK = 128
GROUP_B = 128
N_GROUPS_B = D_B // GROUP_B
N_SUB = 128
DILATIONS = (1, 4, 16)
W_MAX = 2048
ATTN_SCALE = HEAD_DIM ** -0.5
LN_EPS = 1e-5
NEG_INF = -1e30
FFN_HALF = 0.5

LANES = 128
VMEM_LIMIT_BYTES = 56 * 1024 * 1024

TOKEN_TILE = 512
FF_CHUNKS = ((0, 768), (768, 768), (1536, 768), (2304, 512))
ATTN_TILE = 2048
UNITS = ATTN_TILE // N_SUB
HEADS_PER_KV_STEP = 4


def _const_spec(shape):
    nd = len(shape)
    return pl.BlockSpec(shape, lambda *_: (0,) * nd, pipeline_mode=pl.Buffered(1))


def _layer_norm(x, g, b):
    mu = jnp.mean(x, axis=-1, keepdims=True)
    xc = x - mu
    var = jnp.mean(xc * xc, axis=-1, keepdims=True)
    return xc * lax.rsqrt(var + LN_EPS) * g + b


def _rms_norm(x, g):
    return x * lax.rsqrt(jnp.mean(x * x, axis=-1, keepdims=True) + LN_EPS) * g


def _ffn_kernel(*refs, alpha, with_proj):
    if with_proj:
        x_ref, win_ref, wout_ref, g_ref, b_ref, wp_ref, y_ref, h_ref = refs
    else:
        x_ref, win_ref, wout_ref, g_ref, b_ref, y_ref = refs
    x = x_ref[...]
    xb = x.astype(BF16)
    acc = None
    for c0, cw in FF_CHUNKS:
        gate = jnp.dot(xb, win_ref[:, c0:c0 + cw], preferred_element_type=F32)
        up = jnp.dot(xb, win_ref[:, D_FF + c0:D_FF + c0 + cw], preferred_element_type=F32)
        act = (gate * (1.0 / (1.0 + jnp.exp(-gate))) * up).astype(BF16)
        part = jnp.dot(act, wout_ref[c0:c0 + cw, :], preferred_element_type=F32)
        acc = part if acc is None else acc + part
    y = _layer_norm(alpha * x + FFN_HALF * acc, g_ref[...], b_ref[...])
    y_ref[...] = y
    if with_proj:
        h_ref[...] = jnp.dot(y.astype(BF16), wp_ref[...], preferred_element_type=F32)


def _ffn_block(x, w_in, w_out, g, b, w_proj, alpha):
    n = x.shape[0]
    assert n % TOKEN_TILE == 0
    with_proj = w_proj is not None
    row_spec = pl.BlockSpec((TOKEN_TILE, D_MODEL), lambda i: (i, 0))
    in_specs = [row_spec, _const_spec(w_in.shape), _const_spec(w_out.shape),
                _const_spec(g.shape), _const_spec(b.shape)]
    args = [x, w_in, w_out, g, b]
    out_shape = [jax.ShapeDtypeStruct((n, D_MODEL), F32)]
    out_specs = [row_spec]
    if with_proj:
        d_h = w_proj.shape[1]
        in_specs.append(_const_spec(w_proj.shape))
        args.append(w_proj)
        out_shape.append(jax.ShapeDtypeStruct((n, d_h), F32))
        out_specs.append(pl.BlockSpec((TOKEN_TILE, d_h), lambda i: (i, 0)))
    out = pl.pallas_call(
        functools.partial(_ffn_kernel, alpha=alpha, with_proj=with_proj),
        grid=(n // TOKEN_TILE,),
        in_specs=in_specs,
        out_specs=out_specs,
        out_shape=out_shape,
        compiler_params=pltpu.CompilerParams(
            dimension_semantics=("arbitrary",), vmem_limit_bytes=VMEM_LIMIT_BYTES),
        name="ffn_proj_block" if with_proj else "ffn_block",
    )(*args)
    return out if with_proj else out[0]


_STREAM_SLOT0 = (0, 1, 5)
_N_STREAM_SLOTS = 21


def _attn_prompt_kernel(q_ref, k_ref, v_ref, o_ref, kprev, vprev, m_run, l_run, acc_run):
    t = pl.program_id(2)

    @pl.when(t == 0)
    def _():
        kprev[...] = jnp.zeros_like(kprev)
        vprev[...] = jnp.zeros_like(vprev)

    lane = lax.broadcasted_iota(jnp.int32, (N_SUB, LANES), 1)
    head0 = lane < HEAD_DIM
    row = lax.broadcasted_iota(jnp.int32, (N_SUB, 2 * N_SUB), 0)
    col = lax.broadcasted_iota(jnp.int32, (N_SUB, 2 * N_SUB), 1)
    is_prev = col < N_SUB
    rel = (col & (N_SUB - 1)) - row
    signed_rel = jnp.where(is_prev, rel, -rel)

    for bi, d in enumerate(DILATIONS):
        blocks_per_stream = UNITS // d
        shift = int(math.log2(blocks_per_stream))

        def unit(u, carry, bi=bi, d=d, blocks_per_stream=blocks_per_stream, shift=shift):
            r = u >> shift
            n = u & (blocks_per_stream - 1)
            if d == 1:
                idx = pl.ds(pl.multiple_of(u * N_SUB, N_SUB), N_SUB)
            else:
                idx = pl.ds(r + d * N_SUB * n, N_SUB, stride=d)
            slot = _STREAM_SLOT0[bi] + r
            q = q_ref[idx, :] * ATTN_SCALE
            kc = k_ref[idx, :].astype(BF16)
            vc = v_ref[idx, :].astype(BF16)
            kp = kprev[slot]
            vp = vprev[slot]
            kprev[slot] = kc
            vprev[slot] = vc
            zero = jnp.zeros_like(q)
            q2 = jnp.concatenate([jnp.where(head0, q, zero), jnp.where(head0, zero, q)], axis=0).astype(BF16)
            k2 = jnp.concatenate([kp, kc], axis=0)
            v2 = jnp.concatenate([vp, vc], axis=0)
            s = lax.dot_general(q2, k2, (((1,), (1,)), ((), ())), preferred_element_type=F32)
            first = jnp.logical_and(t == 0, n == 0)
            lim = jnp.where(is_prev, jnp.where(first, N_SUB, 0), 0)
            valid = signed_rel >= lim
            valid2 = jnp.concatenate([valid, valid], axis=0)
            s = jnp.where(valid2, s, NEG_INF)
            m = jnp.max(s, axis=-1, keepdims=True)
            p = jnp.exp(s - m)
            l = jnp.sum(p, axis=-1, keepdims=True)
            pv = jnp.dot(p.astype(BF16), v2, preferred_element_type=F32)
            o_un = jnp.where(head0, pv[:N_SUB], pv[N_SUB:])
            mb = jnp.where(head0, m[:N_SUB], m[N_SUB:])
            lb = jnp.where(head0, l[:N_SUB], l[N_SUB:])
            if bi == 0:
                m_run[idx, :] = mb
                l_run[idx, :] = lb
                acc_run[idx, :] = o_un
            else:
                m_old = m_run[idx, :]
                m_new = jnp.maximum(m_old, mb)
                a = jnp.exp(m_old - m_new)
                b = jnp.exp(mb - m_new)
                l_new = a * l_run[idx, :] + b * lb
                acc_new = a * acc_run[idx, :] + b * o_un
                if bi < len(DILATIONS) - 1:
                    m_run[idx, :] = m_new
                    l_run[idx, :] = l_new
                    acc_run[idx, :] = acc_new
                else:
                    o_ref[idx, :] = acc_new / l_new
            return carry

        lax.fori_loop(0, UNITS, unit, 0)


def _attn_prompt(h, batch, seq):
    assert seq % ATTN_TILE == 0
    tiles = seq // ATTN_TILE
    pairs = D_A // LANES
    blk = (ATTN_TILE, LANES)
    return pl.pallas_call(
        _attn_prompt_kernel,
        grid=(batch, pairs, tiles),
        in_specs=[
            pl.BlockSpec(blk, lambda b, hp, t: (b * tiles + t, hp)),
            pl.BlockSpec(blk, lambda b, hp, t: (b * tiles + t, pairs + hp)),
            pl.BlockSpec(blk, lambda b, hp, t: (b * tiles + t, 2 * pairs + hp)),
        ],
        out_specs=pl.BlockSpec(blk, lambda b, hp, t: (b * tiles + t, hp)),
        out_shape=jax.ShapeDtypeStruct((batch * seq, D_A), F32),
        scratch_shapes=[
            pltpu.VMEM((_N_STREAM_SLOTS, N_SUB, LANES), BF16),
            pltpu.VMEM((_N_STREAM_SLOTS, N_SUB, LANES), BF16),
            pltpu.VMEM(blk, F32),
            pltpu.VMEM(blk, F32),
            pltpu.VMEM(blk, F32),
        ],
        compiler_params=pltpu.CompilerParams(
            dimension_semantics=("arbitrary", "arbitrary", "arbitrary"), vmem_limit_bytes=VMEM_LIMIT_BYTES),
        name="attn_prompt",
    )(h, h, h)


def _sample_key_multiplicity(w_buf, n_new):
    cnt = np.zeros((16, w_buf), np.float32)
    j = np.arange(w_buf)
    for t in range(n_new):
        dist = w_buf + t - j
        for d in DILATIONS:
            cnt[t] += ((dist % d == 0) & (dist // d >= 1) & (dist // d <= N_SUB)).astype(np.float32)
    return cnt


def _sample_kv_kernel(q_ref, kn_ref, vn_ref, cnt_ref, kt_ref, vt_ref, o_ref, kto_ref, vto_ref, *, n_new):
    w_buf = kt_ref.shape[-1]
    rows = cnt_ref.shape[0]
    width = q_ref.shape[-1]
    cnt = cnt_ref[...]
    valid = cnt > 0.0
    pad = jnp.zeros((rows - n_new, width), F32)
    q16 = jnp.concatenate([q_ref[...] * ATTN_SCALE, pad], axis=0)
    kn16 = jnp.concatenate([kn_ref[...], pad], axis=0)
    vn16 = jnp.concatenate([vn_ref[...], pad], axis=0)
    tpad = jnp.zeros((LANES - rows, width), F32)
    kn_t = jnp.concatenate([kn16, tpad], axis=0).T
    vn_t = jnp.concatenate([vn16, tpad], axis=0).T
    rr = lax.broadcasted_iota(jnp.int32, (rows, rows), 0)
    cc = lax.broadcasted_iota(jnp.int32, (rows, rows), 1)
    cnt_new = jnp.where(cc == rr, float(len(DILATIONS)), jnp.where(cc < rr, 1.0, 0.0))
    valid_new = cnt_new > 0.0
    tail_lane = lax.broadcasted_iota(jnp.int32, (HEAD_DIM, LANES), 1)
    is_new_lane = tail_lane >= LANES - n_new
    outs = []
    for h in range(HEADS_PER_KV_STEP):
        cols = slice(h * HEAD_DIM, (h + 1) * HEAD_DIM)
        kth = kt_ref[h]
        vth = vt_ref[h]
        qh = q16[:, cols].astype(BF16)
        knh = kn16[:, cols].astype(BF16)
        vnh = vn16[:, cols].astype(BF16)
        s = jnp.dot(qh, kth.astype(BF16), preferred_element_type=F32)
        s_new = lax.dot_general(qh, knh, (((1,), (1,)), ((), ())), preferred_element_type=F32)
        s = jnp.where(valid, s, NEG_INF)
        s_new = jnp.where(valid_new, s_new, NEG_INF)
        m = jnp.maximum(jnp.max(s, axis=-1, keepdims=True), jnp.max(s_new, axis=-1, keepdims=True))
        p = jnp.exp(s - m) * cnt
        p_new = jnp.exp(s_new - m) * cnt_new
        l = jnp.sum(p, axis=-1, keepdims=True) + jnp.sum(p_new, axis=-1, keepdims=True)
        pv = lax.dot_general(p.astype(BF16), vth.astype(BF16), (((1,), (1,)), ((), ())),
                             preferred_element_type=F32)
        pv = pv + jnp.dot(p_new.astype(BF16), vnh, preferred_element_type=F32)
        outs.append((pv / l)[:n_new])
        for src, new_t, dst in ((kth, kn_t, kto_ref), (vth, vn_t, vto_ref)):
            rolled = pltpu.roll(src, w_buf - n_new, 1)
            new_cols = pltpu.roll(new_t[cols, :], LANES - n_new, 1)
            dst[h, :, :w_buf - LANES] = rolled[:, :w_buf - LANES]
            dst[h, :, w_buf - LANES:] = jnp.where(is_new_lane, new_cols, rolled[:, w_buf - LANES:])
    o_ref[...] = jnp.concatenate(outs, axis=-1)


def _sample_kv(h_b, kt, vt):
    dec_batch, n_new, _ = h_b.shape
    w_buf = kt.shape[-1]
    groups = N_HEADS // HEADS_PER_KV_STEP
    width = HEADS_PER_KV_STEP * HEAD_DIM
    cnt = jnp.asarray(_sample_key_multiplicity(w_buf, n_new))
    new_blk = (None, n_new, width)
    kv_blk = (None, HEADS_PER_KV_STEP, HEAD_DIM, w_buf)
    kv_spec = pl.BlockSpec(kv_blk, lambda b, g: (b, g, 0, 0))
    return pl.pallas_call(
        functools.partial(_sample_kv_kernel, n_new=n_new),
        grid=(dec_batch, groups),
        in_specs=[
            pl.BlockSpec(new_blk, lambda b, g: (b, 0, g)),
            pl.BlockSpec(new_blk, lambda b, g: (b, 0, groups + g)),
            pl.BlockSpec(new_blk, lambda b, g: (b, 0, 2 * groups + g)),
            _const_spec(cnt.shape),
            kv_spec,
            kv_spec,
        ],
        out_specs=[pl.BlockSpec(new_blk, lambda b, g: (b, 0, g)), kv_spec, kv_spec],
        out_shape=[
            jax.ShapeDtypeStruct((dec_batch, n_new, D_A), F32),
            jax.ShapeDtypeStruct(kt.shape, F32),
            jax.ShapeDtypeStruct(vt.shape, F32),
        ],
        compiler_params=pltpu.CompilerParams(
            dimension_semantics=("arbitrary", "arbitrary"), vmem_limit_bytes=VMEM_LIMIT_BYTES),
        name="sample_kv",
    )(h_b, h_b, h_b, cnt, kt, vt)


def _mix_kernel(*refs, alpha, sample):
    if sample:
        (x_ref, oa_ref, u_ref, v_ref, coef_ref, sbias_ref, vg_ref, vb_ref, ga_ref, gb_ref,
         wout_ref, g2_ref, b2_ref, x2_ref, vn_ref) = refs
    else:
        (x_ref, oa_ref, u_ref, v_ref, sgw_ref, sbias_ref, vg_ref, vb_ref, ga_ref, gb_ref,
         wout_ref, g2_ref, b2_ref, x2_ref) = refs
    vn = _layer_norm(v_ref[...], vg_ref[...], vb_ref[...])
    n_chunks = vn.shape[0] // CHUNK
    if sample:
        vn_ref[...] = vn
        blocks = []
        for ti in range(n_chunks):
            acc = sbias_ref[ti:ti + 1, :]
            for j in range(ti + 1):
                acc = acc + coef_ref[ti * n_chunks + j:ti * n_chunks + j + 1, :] * vn[j * CHUNK:(j + 1) * CHUNK, :]
            blocks.append(acc)
        mix = jnp.concatenate(blocks, axis=0)
    else:
        ri = lax.broadcasted_iota(jnp.int32, (CHUNK, CHUNK), 0)
        ci = lax.broadcasted_iota(jnp.int32, (CHUNK, CHUNK), 1)
        causal = ci <= ri
        vnb = vn.astype(BF16)
        per_group = []
        for g in range(N_GROUPS_B):
            w = jnp.where(causal, sgw_ref[g], 0.0).astype(BF16)
            lanes = slice(g * GROUP_B, (g + 1) * GROUP_B)
            rhs = jnp.concatenate([vnb[c * CHUNK:(c + 1) * CHUNK, lanes] for c in range(n_chunks)], axis=1)
            per_group.append(jnp.dot(w, rhs, preferred_element_type=F32))
        mix = jnp.concatenate(
            [jnp.concatenate([per_group[g][:, c * CHUNK:(c + 1) * CHUNK] for g in range(N_GROUPS_B)], axis=1)
             + sbias_ref[...] for c in range(n_chunks)], axis=0)
    o_b = u_ref[...] * mix
    cat = jnp.concatenate([_rms_norm(oa_ref[...], ga_ref[...]), _rms_norm(o_b, gb_ref[...])], axis=-1)
    mixed = jnp.dot(cat.astype(BF16), wout_ref[...], preferred_element_type=F32)
    x2_ref[...] = _layer_norm(alpha * x_ref[...] + mixed, g2_ref[...], b2_ref[...])


def _mix_block(x1, o_a, h, sg_w, sg_bias, vg, vb, ga, gb, w_out, g2, b2, alpha, sample):
    n = x1.shape[0]
    assert n % TOKEN_TILE == 0
    u_blk = 3 * D_A // D_B
    row = lambda width, j: pl.BlockSpec((TOKEN_TILE, width), lambda i, j=j: (i, j))
    consts = [sg_w, sg_bias, vg, vb, ga, gb, w_out, g2, b2]
    in_specs = [row(D_MODEL, 0), row(D_A, 0), row(D_B, u_blk), row(D_B, u_blk + 1)]
    in_specs += [_const_spec(c.shape) for c in consts]
    out_shape = [jax.ShapeDtypeStruct((n, D_MODEL), F32)]
    out_specs = [row(D_MODEL, 0)]
    if sample:
        out_shape.append(jax.ShapeDtypeStruct((n, D_B), F32))
        out_specs.append(row(D_B, 0))
    out = pl.pallas_call(
        functools.partial(_mix_kernel, alpha=alpha, sample=sample),
        grid=(n // TOKEN_TILE,),
        in_specs=in_specs,
        out_specs=out_specs,
        out_shape=out_shape,
        compiler_params=pltpu.CompilerParams(
            dimension_semantics=("arbitrary",), vmem_limit_bytes=VMEM_LIMIT_BYTES),
        name="mix_block_sample" if sample else "mix_block",
    )(x1, o_a, h, h, *consts)
    return out if sample else out[0]


def kernel(x_prompt, x_sample, cache_k, cache_v, ffn1_w_in, ffn1_w_out, ln1_g, ln1_b, w_in, sgu_w, sgu_b,
           sgu_v_g, sgu_v_b, out_a_g, out_b_g, w_out, ln2_g, ln2_b, ffn2_w_in, ffn2_w_out, ln3_g, ln3_b):
    depth = ffn1_w_in.shape[0]
    batch, seq, _ = x_prompt.shape
    dec_batch, n_new, _ = x_sample.shape
    w_buf = cache_k.shape[2]
    assert TOKEN_TILE == n_new * CHUNK and dec_batch == CHUNK and seq % ATTN_TILE == 0 and w_buf == W_MAX
    alpha = (2.0 * depth) ** 0.25
    keep = min(W_MAX, seq)

    xp = x_prompt.reshape(batch * seq, D_MODEL)
    xs = jnp.transpose(x_sample, (1, 0, 2)).reshape(n_new * dec_batch, D_MODEL)
    kp_l, vp_l, ks_l, vs_l, us_l = [], [], [], [], []
    tril = jnp.tril(jnp.ones((CHUNK, CHUNK), bool))
    for layer in range(depth):
        w1i, w1o = ffn1_w_in[layer].astype(BF16), ffn1_w_out[layer].astype(BF16)
        w2i, w2o = ffn2_w_in[layer].astype(BF16), ffn2_w_out[layer].astype(BF16)
        wp, wo = w_in[layer].astype(BF16), w_out[layer].astype(BF16)
        row = lambda a: a[layer][None, :]
        g1, b1, g2, b2, g3, b3 = row(ln1_g), row(ln1_b), row(ln2_g), row(ln2_b), row(ln3_g), row(ln3_b)
        vg, vb, ga, gb = row(sgu_v_g), row(sgu_v_b), row(out_a_g), row(out_b_g)
        sgw = sgu_w[layer]
        sg_bias = jnp.repeat(sgu_b[layer].T, GROUP_B, axis=1)
        corner = jnp.where(tril[:n_new, :n_new], sgw[:, :n_new, :n_new], 0.0)
        coef = jnp.repeat(jnp.transpose(corner, (1, 2, 0)).reshape(n_new * n_new, N_GROUPS_B), GROUP_B, axis=1)

        x1, h = _ffn_block(xp, w1i, w1o, g1, b1, wp, alpha)
        o_a = _attn_prompt(h, batch, seq)
        x2 = _mix_block(x1, o_a, h, sgw, sg_bias, vg, vb, ga, gb, wo, g2, b2, alpha, sample=False)
        xp = _ffn_block(x2, w2i, w2o, g3, b3, None, alpha)
        h3 = h.reshape(batch, seq, -1)
        kp_l.append(h3[:, seq - keep:, D_A:2 * D_A].reshape(batch, keep, N_HEADS, HEAD_DIM))
        vp_l.append(h3[:, seq - keep:, 2 * D_A:3 * D_A].reshape(batch, keep, N_HEADS, HEAD_DIM))

        x1s, hs = _ffn_block(xs, w1i, w1o, g1, b1, wp, alpha)
        hs_b = jnp.transpose(hs.reshape(n_new, dec_batch, -1), (1, 0, 2))
        kt = jnp.transpose(cache_k[layer], (0, 2, 3, 1))
        vt = jnp.transpose(cache_v[layer], (0, 2, 3, 1))
        o_s, kt_new, vt_new = _sample_kv(hs_b, kt, vt)
        o_s = jnp.transpose(o_s, (1, 0, 2)).reshape(n_new * dec_batch, D_A)
        x2s, vn_s = _mix_block(x1s, o_s, hs, coef, sg_bias[:n_new], vg, vb, ga, gb, wo, g2, b2, alpha, sample=True)
        xs = _ffn_block(x2s, w2i, w2o, g3, b3, None, alpha)
        ks_l.append(jnp.transpose(kt_new, (0, 3, 1, 2)))
        vs_l.append(jnp.transpose(vt_new, (0, 3, 1, 2)))
        us_l.append(jnp.transpose(vn_s.reshape(n_new, dec_batch, D_B), (1, 0, 2)))

    y_p = xp.reshape(batch, seq, D_MODEL)
    y_s = jnp.transpose(xs.reshape(n_new, dec_batch, D_MODEL), (1, 0, 2))
    return (y_p, y_s, jnp.stack(kp_l), jnp.stack(vp_l), jnp.stack(ks_l), jnp.stack(vs_l), jnp.stack(us_l))
```

```python
import functools
import math

import numpy as np
import jax
import jax.numpy as jnp
from jax import lax
from jax.experimental import pallas as pl
from jax.experimental.pallas import tpu as pltpu

F32 = jnp.float32
BF16 = jnp.bfloat16

D_MODEL = 1024
HEAD_DIM = 64
N_HEADS = 8
D_A = N_HEADS * HEAD_DIM
D_B = D_MODEL - D_A
D_FF = 2816
CHUNK = 128
GROUP_B = 128
N_GROUPS_B = D_B // GROUP_B
N_SUB = 128
DILATIONS = (1, 4, 16)
W_MAX = 2048
ATTN_SCALE = HEAD_DIM ** -0.5
LN_EPS = 1e-5
NEG_INF = -1e30
FFN_HALF = 0.5

LANES = 128
VMEM_LIMIT_BYTES = 56 * 1024 * 1024

TOKEN_TILE = 512
FF_CHUNKS = ((0, 768), (768, 768), (1536, 768), (2304, 512))
ATTN_TILE = 2048
UNITS = ATTN_TILE // N_SUB
ATTN_UNROLL = 16
HEADS_PER_KV_STEP = 4


def _const_spec(shape):
    nd = len(shape)
    return pl.BlockSpec(shape, lambda *_: (0,) * nd, pipeline_mode=pl.Buffered(1))


def _layer_norm(x, g, b):
    mu = jnp.mean(x, axis=-1, keepdims=True)
    xc = x - mu
    var = jnp.mean(xc * xc, axis=-1, keepdims=True)
    return xc * lax.rsqrt(var + LN_EPS) * g + b


def _rms_norm(x, g):
    return x * lax.rsqrt(jnp.mean(x * x, axis=-1, keepdims=True) + LN_EPS) * g


def _ffn_half_step(x, win_ref, wout_ref, g_ref, b_ref, alpha):
    xb = x.astype(BF16)
    acc = None
    for c0, cw in FF_CHUNKS:
        gate = jnp.dot(xb, win_ref[:, c0:c0 + cw], preferred_element_type=F32)
        up = jnp.dot(xb, win_ref[:, D_FF + c0:D_FF + c0 + cw], preferred_element_type=F32)
        act = (gate * (1.0 / (1.0 + jnp.exp(-gate))) * up).astype(BF16)
        part = jnp.dot(act, wout_ref[c0:c0 + cw, :], preferred_element_type=F32)
        acc = part if acc is None else acc + part
    return _layer_norm(alpha * x + FFN_HALF * acc, g_ref[...], b_ref[...])


def _ffn_proj_kernel(x_ref, win_ref, wout_ref, g_ref, b_ref, wp_ref, y_ref, h_ref, *, alpha):
    y = _ffn_half_step(x_ref[...], win_ref, wout_ref, g_ref, b_ref, alpha)
    y_ref[...] = y
    h_ref[...] = jnp.dot(y.astype(BF16), wp_ref[...], preferred_element_type=F32)


def _ffn_proj_block(x, w_in, w_out, g, b, w_proj, alpha):
    n = x.shape[0]
    assert n % TOKEN_TILE == 0
    d_h = w_proj.shape[1]
    row_spec = pl.BlockSpec((TOKEN_TILE, D_MODEL), lambda i: (i, 0))
    consts = [w_in, w_out, g, b, w_proj]
    return pl.pallas_call(
        functools.partial(_ffn_proj_kernel, alpha=alpha),
        grid=(n // TOKEN_TILE,),
        in_specs=[row_spec] + [_const_spec(c.shape) for c in consts],
        out_specs=[row_spec, pl.BlockSpec((TOKEN_TILE, d_h), lambda i: (i, 0))],
        out_shape=[jax.ShapeDtypeStruct((n, D_MODEL), F32), jax.ShapeDtypeStruct((n, d_h), F32)],
        compiler_params=pltpu.CompilerParams(
            dimension_semantics=("arbitrary",), vmem_limit_bytes=VMEM_LIMIT_BYTES),
        name="ffn_proj_block",
    )(x, *consts)


_STREAM_SLOT0 = (0, 1, 5)
_N_STREAM_SLOTS = 21


def _attn_prompt_kernel(q_ref, k_ref, v_ref, o_ref, kprev, vprev, bias_scr, m_run, l_run, acc_run):
    t = pl.program_id(2)

    @pl.when(t == 0)
    def _():
        kprev[...] = jnp.zeros_like(kprev)
        vprev[...] = jnp.zeros_like(vprev)

    lane = lax.broadcasted_iota(jnp.int32, (N_SUB, LANES), 1)
    head0 = lane < HEAD_DIM
    key = lax.broadcasted_iota(jnp.int32, (2 * N_SUB, N_SUB), 0)
    qry = lax.broadcasted_iota(jnp.int32, (2 * N_SUB, N_SUB), 1)
    is_prev = key < N_SUB
    valid = jnp.logical_or(jnp.logical_and(is_prev, key >= qry),
                           jnp.logical_and(jnp.logical_not(is_prev), (key - N_SUB) <= qry))
    neg = jnp.full((2 * N_SUB, N_SUB), NEG_INF, F32)
    zeros = jnp.zeros((2 * N_SUB, N_SUB), F32)
    bias_scr[0] = jnp.where(valid, zeros, neg).astype(BF16)
    bias_scr[1] = jnp.where(jnp.logical_and(valid, jnp.logical_not(is_prev)), zeros, neg).astype(BF16)
    one_hot = ((key & (N_SUB - 1)) == qry).astype(F32).astype(BF16)
    ones_blk = jnp.ones((2 * N_SUB, LANES), BF16)

    for bi, d in enumerate(DILATIONS):
        blocks_per_stream = UNITS // d
        shift = int(math.log2(blocks_per_stream))

        def unit(u, carry, bi=bi, d=d, blocks_per_stream=blocks_per_stream, shift=shift):
            r = u >> shift
            n = u & (blocks_per_stream - 1)
            if d == 1:
                idx = pl.ds(pl.multiple_of(u * N_SUB, N_SUB), N_SUB)
            else:
                idx = pl.ds(r + d * N_SUB * n, N_SUB, stride=d)
            slot = _STREAM_SLOT0[bi] + r
            q = q_ref[idx, :] * ATTN_SCALE
            kc = k_ref[idx, :].astype(BF16)
            vc = v_ref[idx, :].astype(BF16)
            kp = kprev[slot]
            vp = vprev[slot]
            kprev[slot] = kc
            vprev[slot] = vc
            zero = jnp.zeros_like(q)
            q2 = jnp.concatenate([jnp.where(head0, q, zero), jnp.where(head0, zero, q)], axis=0).astype(BF16)
            first = jnp.logical_and(t == 0, n == 0).astype(jnp.int32)
            q_aug = jnp.concatenate([q2, one_hot], axis=1)
            k_aug = jnp.concatenate([jnp.concatenate([kp, kc], axis=0), bias_scr[first]], axis=1)
            v_aug = jnp.concatenate([jnp.concatenate([vp, vc], axis=0), ones_blk], axis=1)
            s = lax.dot_general(q_aug, k_aug, (((1,), (1,)), ((), ())), preferred_element_type=F32)
            m = jnp.max(s, axis=-1, keepdims=True)
            p = jnp.exp(s - m)
            pv = jnp.dot(p.astype(BF16), v_aug, preferred_element_type=F32)
            o_un = jnp.where(head0, pv[:N_SUB, :LANES], pv[N_SUB:, :LANES])
            mb = jnp.where(head0, m[:N_SUB], m[N_SUB:])
            lb = jnp.where(head0, pv[:N_SUB, LANES:], pv[N_SUB:, LANES:])
            if bi == 0:
                m_run[idx, :] = mb
                l_run[idx, :] = lb
                acc_run[idx, :] = o_un
            else:
                m_old = m_run[idx, :]
                m_new = jnp.maximum(m_old, mb)
                a = jnp.exp(m_old - m_new)
                b = jnp.exp(mb - m_new)
                l_new = a * l_run[idx, :] + b * lb
                acc_new = a * acc_run[idx, :] + b * o_un
                if bi < len(DILATIONS) - 1:
                    m_run[idx, :] = m_new
                    l_run[idx, :] = l_new
                    acc_run[idx, :] = acc_new
                else:
                    o_ref[idx, :] = acc_new / l_new
            return carry

        lax.fori_loop(0, UNITS, unit, 0, unroll=ATTN_UNROLL)


def _attn_prompt(h, batch, seq):
    assert seq % ATTN_TILE == 0
    tiles = seq // ATTN_TILE
    pairs = D_A // LANES
    blk = (ATTN_TILE, LANES)
    return pl.pallas_call(
        _attn_prompt_kernel,
        grid=(batch, pairs, tiles),
        in_specs=[
            pl.BlockSpec(blk, lambda b, hp, t: (b * tiles + t, hp)),
            pl.BlockSpec(blk, lambda b, hp, t: (b * tiles + t, pairs + hp)),
            pl.BlockSpec(blk, lambda b, hp, t: (b * tiles + t, 2 * pairs + hp)),
        ],
        out_specs=pl.BlockSpec(blk, lambda b, hp, t: (b * tiles + t, hp)),
        out_shape=jax.ShapeDtypeStruct((batch * seq, D_A), F32),
        scratch_shapes=[
            pltpu.VMEM((_N_STREAM_SLOTS, N_SUB, LANES), BF16),
            pltpu.VMEM((_N_STREAM_SLOTS, N_SUB, LANES), BF16),
            pltpu.VMEM((2, 2 * N_SUB, N_SUB), BF16),
            pltpu.VMEM(blk, F32),
            pltpu.VMEM(blk, F32),
            pltpu.VMEM(blk, F32),
        ],
        compiler_params=pltpu.CompilerParams(
            dimension_semantics=("arbitrary", "arbitrary", "arbitrary"), vmem_limit_bytes=VMEM_LIMIT_BYTES),
        name="attn_prompt",
    )(h, h, h)


def _sample_key_multiplicity(w_buf, n_new):
    cnt = np.zeros((16, w_buf), np.float32)
    j = np.arange(w_buf)
    for t in range(n_new):
        dist = w_buf + t - j
        for d in DILATIONS:
            cnt[t] += ((dist % d == 0) & (dist // d >= 1) & (dist // d <= N_SUB)).astype(np.float32)
    return cnt


def _sample_kv_kernel(q_ref, kn_ref, vn_ref, cnt_ref, kt_ref, vt_ref, o_ref, kto_ref, vto_ref, *, n_new):
    w_buf = kt_ref.shape[-1]
    rows = cnt_ref.shape[0]
    width = q_ref.shape[-1]
    cnt = cnt_ref[...]
    valid = cnt > 0.0
    pad = jnp.zeros((rows - n_new, width), F32)
    q16 = jnp.concatenate([q_ref[...] * ATTN_SCALE, pad], axis=0)
    kn16 = jnp.concatenate([kn_ref[...], pad], axis=0)
    vn16 = jnp.concatenate([vn_ref[...], pad], axis=0)
    tpad = jnp.zeros((LANES - rows, width), F32)
    kn_t = jnp.concatenate([kn16, tpad], axis=0).T
    vn_t = jnp.concatenate([vn16, tpad], axis=0).T
    rr = lax.broadcasted_iota(jnp.int32, (rows, rows), 0)
    cc = lax.broadcasted_iota(jnp.int32, (rows, rows), 1)
    cnt_new = jnp.where(cc == rr, float(len(DILATIONS)), jnp.where(cc < rr, 1.0, 0.0))
    valid_new = cnt_new > 0.0
    tail_lane = lax.broadcasted_iota(jnp.int32, (HEAD_DIM, LANES), 1)
    is_new_lane = tail_lane >= LANES - n_new
    outs = []
    for h in range(HEADS_PER_KV_STEP):
        cols = slice(h * HEAD_DIM, (h + 1) * HEAD_DIM)
        kth = kt_ref[h]
        vth = vt_ref[h]
        qh = q16[:, cols].astype(BF16)
        knh = kn16[:, cols].astype(BF16)
        vnh = vn16[:, cols].astype(BF16)
        s = jnp.dot(qh, kth.astype(BF16), preferred_element_type=F32)
        s_new = lax.dot_general(qh, knh, (((1,), (1,)), ((), ())), preferred_element_type=F32)
        s = jnp.where(valid, s, NEG_INF)
        s_new = jnp.where(valid_new, s_new, NEG_INF)
        m = jnp.maximum(jnp.max(s, axis=-1, keepdims=True), jnp.max(s_new, axis=-1, keepdims=True))
        p = jnp.exp(s - m) * cnt
        p_new = jnp.exp(s_new - m) * cnt_new
        l = jnp.sum(p, axis=-1, keepdims=True) + jnp.sum(p_new, axis=-1, keepdims=True)
        pv = lax.dot_general(p.astype(BF16), vth.astype(BF16), (((1,), (1,)), ((), ())),
                             preferred_element_type=F32)
        pv = pv + jnp.dot(p_new.astype(BF16), vnh, preferred_element_type=F32)
        outs.append((pv / l)[:n_new])
        for src, new_t, dst in ((kth, kn_t, kto_ref), (vth, vn_t, vto_ref)):
            rolled = pltpu.roll(src, w_buf - n_new, 1)
            new_cols = pltpu.roll(new_t[cols, :], LANES - n_new, 1)
            dst[h, :, :w_buf - LANES] = rolled[:, :w_buf - LANES]
            dst[h, :, w_buf - LANES:] = jnp.where(is_new_lane, new_cols, rolled[:, w_buf - LANES:])
    o_ref[...] = jnp.concatenate(outs, axis=-1)


def _sample_kv(h_b, kt, vt):
    dec_batch, n_new, _ = h_b.shape
    w_buf = kt.shape[-1]
    groups = N_HEADS // HEADS_PER_KV_STEP
    width = HEADS_PER_KV_STEP * HEAD_DIM
    cnt = jnp.asarray(_sample_key_multiplicity(w_buf, n_new))
    new_blk = (None, n_new, width)
    kv_blk = (None, HEADS_PER_KV_STEP, HEAD_DIM, w_buf)
    kv_spec = pl.BlockSpec(kv_blk, lambda b, g: (b, g, 0, 0))
    return pl.pallas_call(
        functools.partial(_sample_kv_kernel, n_new=n_new),
        grid=(dec_batch, groups),
        in_specs=[
            pl.BlockSpec(new_blk, lambda b, g: (b, 0, g)),
            pl.BlockSpec(new_blk, lambda b, g: (b, 0, groups + g)),
            pl.BlockSpec(new_blk, lambda b, g: (b, 0, 2 * groups + g)),
            _const_spec(cnt.shape),
            kv_spec,
            kv_spec,
        ],
        out_specs=[pl.BlockSpec(new_blk, lambda b, g: (b, 0, g)), kv_spec, kv_spec],
        out_shape=[
            jax.ShapeDtypeStruct((dec_batch, n_new, D_A), F32),
            jax.ShapeDtypeStruct(kt.shape, F32),
            jax.ShapeDtypeStruct(vt.shape, F32),
        ],
        compiler_params=pltpu.CompilerParams(
            dimension_semantics=("arbitrary", "arbitrary"), vmem_limit_bytes=VMEM_LIMIT_BYTES),
        name="sample_kv",
    )(h_b, h_b, h_b, cnt, kt, vt)


def _mix_ffn_kernel(*refs, alpha, sample):
    if sample:
        (x_ref, oa_ref, u_ref, v_ref, coef_ref, sbias_ref, vg_ref, vb_ref, ga_ref, gb_ref,
         wout_ref, g2_ref, b2_ref, win2_ref, wout2_ref, g3_ref, b3_ref, y_ref, vn_ref) = refs
    else:
        (x_ref, oa_ref, u_ref, v_ref, sgw_ref, sbias_ref, vg_ref, vb_ref, ga_ref, gb_ref,
         wout_ref, g2_ref, b2_ref, win2_ref, wout2_ref, g3_ref, b3_ref, y_ref) = refs
    vn = _layer_norm(v_ref[...], vg_ref[...], vb_ref[...])
    n_chunks = vn.shape[0] // CHUNK
    if sample:
        vn_ref[...] = vn
        blocks = []
        for ti in range(n_chunks):
            acc = sbias_ref[ti:ti + 1, :]
            for j in range(ti + 1):
                acc = acc + coef_ref[ti * n_chunks + j:ti * n_chunks + j + 1, :] * vn[j * CHUNK:(j + 1) * CHUNK, :]
            blocks.append(acc)
        mix = jnp.concatenate(blocks, axis=0)
    else:
        ri = lax.broadcasted_iota(jnp.int32, (CHUNK, CHUNK), 0)
        ci = lax.broadcasted_iota(jnp.int32, (CHUNK, CHUNK), 1)
        causal = ci <= ri
        vnb = vn.astype(BF16)
        per_group = []
        for g in range(N_GROUPS_B):
            w = jnp.where(causal, sgw_ref[g], 0.0).astype(BF16)
            lanes = slice(g * GROUP_B, (g + 1) * GROUP_B)
            rhs = jnp.concatenate([vnb[c * CHUNK:(c + 1) * CHUNK, lanes] for c in range(n_chunks)], axis=1)
            per_group.append(jnp.dot(w, rhs, preferred_element_type=F32))
        mix = jnp.concatenate(
            [jnp.concatenate([per_group[g][:, c * CHUNK:(c + 1) * CHUNK] for g in range(N_GROUPS_B)], axis=1)
             + sbias_ref[...] for c in range(n_chunks)], axis=0)
    o_b = u_ref[...] * mix
    cat = jnp.concatenate([_rms_norm(oa_ref[...], ga_ref[...]), _rms_norm(o_b, gb_ref[...])], axis=-1)
    mixed = jnp.dot(cat.astype(BF16), wout_ref[...], preferred_element_type=F32)
    x2 = _layer_norm(alpha * x_ref[...] + mixed, g2_ref[...], b2_ref[...])
    y_ref[...] = _ffn_half_step(x2, win2_ref, wout2_ref, g3_ref, b3_ref, alpha)


def _mix_ffn_block(x1, o_a, h, sg_w, sg_bias, vg, vb, ga, gb, w_out, g2, b2, w2_in, w2_out, g3, b3, alpha, sample):
    n = x1.shape[0]
    assert n % TOKEN_TILE == 0
    u_blk = 3 * D_A // D_B
    row = lambda width, j: pl.BlockSpec((TOKEN_TILE, width), lambda i, j=j: (i, j))
    consts = [sg_w, sg_bias, vg, vb, ga, gb, w_out, g2, b2, w2_in, w2_out, g3, b3]
    in_specs = [row(D_MODEL, 0), row(D_A, 0), row(D_B, u_blk), row(D_B, u_blk + 1)]
    in_specs += [_const_spec(c.shape) for c in consts]
    out_shape = [jax.ShapeDtypeStruct((n, D_MODEL), F32)]
    out_specs = [row(D_MODEL, 0)]
    if sample:
        out_shape.append(jax.ShapeDtypeStruct((n, D_B), F32))
        out_specs.append(row(D_B, 0))
    out = pl.pallas_call(
        functools.partial(_mix_ffn_kernel, alpha=alpha, sample=sample),
        grid=(n // TOKEN_TILE,),
        in_specs=in_specs,
        out_specs=out_specs,
        out_shape=out_shape,
        compiler_params=pltpu.CompilerParams(
            dimension_semantics=("arbitrary",), vmem_limit_bytes=VMEM_LIMIT_BYTES),
        name="mix_ffn_block_sample" if sample else "mix_ffn_block",
    )(x1, o_a, h, h, *consts)
    return out if sample else out[0]


def kernel(x_prompt, x_sample, cache_k, cache_v, ffn1_w_in, ffn1_w_out, ln1_g, ln1_b, w_in, sgu_w, sgu_b,
           sgu_v_g, sgu_v_b, out_a_g, out_b_g, w_out, ln2_g, ln2_b, ffn2_w_in, ffn2_w_out, ln3_g, ln3_b):
    depth = ffn1_w_in.shape[0]
    batch, seq, _ = x_prompt.shape
    dec_batch, n_new, _ = x_sample.shape
    w_buf = cache_k.shape[2]
    assert TOKEN_TILE == n_new * CHUNK and dec_batch == CHUNK and seq % ATTN_TILE == 0 and w_buf == W_MAX
    alpha = (2.0 * depth) ** 0.25
    keep = min(W_MAX, seq)

    xp = x_prompt.reshape(batch * seq, D_MODEL)
    xs = jnp.transpose(x_sample, (1, 0, 2)).reshape(n_new * dec_batch, D_MODEL)
    kp_l, vp_l, ks_l, vs_l, us_l = [], [], [], [], []
    tril = jnp.tril(jnp.ones((CHUNK, CHUNK), bool))
    for layer in range(depth):
        w1i, w1o = ffn1_w_in[layer].astype(BF16), ffn1_w_out[layer].astype(BF16)
        w2i, w2o = ffn2_w_in[layer].astype(BF16), ffn2_w_out[layer].astype(BF16)
        wp, wo = w_in[layer].astype(BF16), w_out[layer].astype(BF16)
        row = lambda a: a[layer][None, :]
        g1, b1, g2, b2, g3, b3 = row(ln1_g), row(ln1_b), row(ln2_g), row(ln2_b), row(ln3_g), row(ln3_b)
        vg, vb, ga, gb = row(sgu_v_g), row(sgu_v_b), row(out_a_g), row(out_b_g)
        sgw = sgu_w[layer]
        sg_bias = jnp.repeat(sgu_b[layer].T, GROUP_B, axis=1)
        corner = jnp.where(tril[:n_new, :n_new], sgw[:, :n_new, :n_new], 0.0)
        coef = jnp.repeat(jnp.transpose(corner, (1, 2, 0)).reshape(n_new * n_new, N_GROUPS_B), GROUP_B, axis=1)

        x1, h = _ffn_proj_block(xp, w1i, w1o, g1, b1, wp, alpha)
        o_a = _attn_prompt(h, batch, seq)
        xp = _mix_ffn_block(x1, o_a, h, sgw, sg_bias, vg, vb, ga, gb, wo, g2, b2, w2i, w2o, g3, b3, alpha,
                            sample=False)
        h3 = h.reshape(batch, seq, -1)
        kp_l.append(h3[:, seq - keep:, D_A:2 * D_A].reshape(batch, keep, N_HEADS, HEAD_DIM))
        vp_l.append(h3[:, seq - keep:, 2 * D_A:3 * D_A].reshape(batch, keep, N_HEADS, HEAD_DIM))

        x1s, hs = _ffn_proj_block(xs, w1i, w1o, g1, b1, wp, alpha)
        hs_b = jnp.transpose(hs.reshape(n_new, dec_batch, -1), (1, 0, 2))
        kt = jnp.transpose(cache_k[layer], (0, 2, 3, 1))
        vt = jnp.transpose(cache_v[layer], (0, 2, 3, 1))
        o_s, kt_new, vt_new = _sample_kv(hs_b, kt, vt)
        o_s = jnp.transpose(o_s, (1, 0, 2)).reshape(n_new * dec_batch, D_A)
        xs, vn_s = _mix_ffn_block(x1s, o_s, hs, coef, sg_bias[:n_new], vg, vb, ga, gb, wo, g2, b2, w2i, w2o, g3, b3,
                                  alpha, sample=True)
        ks_l.append(jnp.transpose(kt_new, (0, 3, 1, 2)))
        vs_l.append(jnp.transpose(vt_new, (0, 3, 1, 2)))
        us_l.append(jnp.transpose(vn_s.reshape(n_new, dec_batch, D_B), (1, 0, 2)))

    y_p = xp.reshape(batch, seq, D_MODEL)
    y_s = jnp.transpose(xs.reshape(n_new, dec_batch, D_MODEL), (1, 0, 2))
    return (y_p, y_s, jnp.stack(kp_l), jnp.stack(vp_l), jnp.stack(ks_l), jnp.stack(vs_l), jnp.stack(us_l))
```

```python
import functools
import math

import numpy as np
import jax
import jax.numpy as jnp
from jax import lax
from jax.experimental import pallas as pl
from jax.experimental.pallas import tpu as pltpu

F32 = jnp.float32
BF16 = jnp.bfloat16

D_MODEL = 1024
HEAD_DIM = 64
N_HEADS = 8
D_A = N_HEADS * HEAD_DIM
D_B = D_MODEL - D_A
D_FF = 2816
CHUNK = 128
GROUP_B = 128
N_GROUPS_B = D_B // GROUP_B
N_SUB = 128
DILATIONS = (1, 4, 16)
W_MAX = 2048
ATTN_SCALE = HEAD_DIM ** -0.5
LOG2_E = math.log2(math.e)
LN_EPS = 1e-5
NEG_INF = -1e30
FFN_HALF = 0.5

LANES = 128
VMEM_LIMIT_BYTES = 56 * 1024 * 1024

TOKEN_TILE = 512
TAIL_TILE = 256
FF_CHUNKS = ((0, 768), (768, 768), (1536, 768), (2304, 512))
ATTN_TILE = 2048
UNITS = ATTN_TILE // N_SUB
ATTN_UNROLL = 16
HEADS_PER_KV_STEP = 4


def _const_spec(shape):
    nd = len(shape)
    return pl.BlockSpec(shape, lambda *_: (0,) * nd, pipeline_mode=pl.Buffered(1))


def _layer_norm(x, g, b):
    mu = jnp.mean(x, axis=-1, keepdims=True)
    xc = x - mu
    var = jnp.mean(xc * xc, axis=-1, keepdims=True)
    return xc * lax.rsqrt(var + LN_EPS) * g + b


def _rms_norm(x, g):
    return x * lax.rsqrt(jnp.mean(x * x, axis=-1, keepdims=True) + LN_EPS) * g


def _ffn_half_step(x, win_ref, wout_ref, g_ref, b_ref, alpha):
    xb = x.astype(BF16)
    acc = None
    for c0, cw in FF_CHUNKS:
        gate = jnp.dot(xb, win_ref[:, c0:c0 + cw], preferred_element_type=F32)
        up = jnp.dot(xb, win_ref[:, D_FF + c0:D_FF + c0 + cw], preferred_element_type=F32)
        act = (gate * (1.0 / (1.0 + jnp.exp(-gate))) * up).astype(BF16)
        part = jnp.dot(act, wout_ref[c0:c0 + cw, :], preferred_element_type=F32)
        acc = part if acc is None else acc + part
    return _layer_norm(alpha * x + FFN_HALF * acc, g_ref[...], b_ref[...])


def _ffn_proj_kernel(x_ref, win_ref, wout_ref, g_ref, b_ref, wp_ref, y_ref, h_ref, *, alpha):
    y = _ffn_half_step(x_ref[...], win_ref, wout_ref, g_ref, b_ref, alpha)
    y_ref[...] = y
    h_ref[...] = jnp.dot(y.astype(BF16), wp_ref[...], preferred_element_type=F32)


def _ffn_proj_block(x, w_in, w_out, g, b, w_proj, alpha):
    n = x.shape[0]
    assert n % TOKEN_TILE == 0
    d_h = w_proj.shape[1]
    row_spec = pl.BlockSpec((TOKEN_TILE, D_MODEL), lambda i: (i, 0))
    consts = [w_in, w_out, g, b, w_proj]
    return pl.pallas_call(
        functools.partial(_ffn_proj_kernel, alpha=alpha),
        grid=(n // TOKEN_TILE,),
        in_specs=[row_spec] + [_const_spec(c.shape) for c in consts],
        out_specs=[row_spec, pl.BlockSpec((TOKEN_TILE, d_h), lambda i: (i, 0))],
        out_shape=[jax.ShapeDtypeStruct((n, D_MODEL), F32), jax.ShapeDtypeStruct((n, d_h), F32)],
        compiler_params=pltpu.CompilerParams(
            dimension_semantics=("arbitrary",), vmem_limit_bytes=VMEM_LIMIT_BYTES),
        name="ffn_proj_block",
    )(x, *consts)


_STREAM_SLOT0 = (0, 1, 5)
_N_STREAM_SLOTS = 21


def _attn_prompt_kernel(q_ref, k_ref, v_ref, o_ref, kprev, vprev, bias_scr, m_run, l_run, acc_run):
    t = pl.program_id(2)

    @pl.when(t == 0)
    def _():
        kprev[...] = jnp.zeros_like(kprev)
        vprev[...] = jnp.zeros_like(vprev)

    lane = lax.broadcasted_iota(jnp.int32, (N_SUB, LANES), 1)
    head0 = lane < HEAD_DIM
    key = lax.broadcasted_iota(jnp.int32, (2 * N_SUB, N_SUB), 0)
    qry = lax.broadcasted_iota(jnp.int32, (2 * N_SUB, N_SUB), 1)
    is_prev = key < N_SUB
    valid = jnp.logical_or(jnp.logical_and(is_prev, key >= qry),
                           jnp.logical_and(jnp.logical_not(is_prev), (key - N_SUB) <= qry))
    neg = jnp.full((2 * N_SUB, N_SUB), NEG_INF, F32)
    zeros = jnp.zeros((2 * N_SUB, N_SUB), F32)
    bias_scr[0] = jnp.where(valid, zeros, neg).astype(BF16)
    bias_scr[1] = jnp.where(jnp.logical_and(valid, jnp.logical_not(is_prev)), zeros, neg).astype(BF16)
    one_hot = ((key & (N_SUB - 1)) == qry).astype(F32).astype(BF16)
    ones_blk = jnp.ones((2 * N_SUB, LANES), BF16)

    for bi, d in enumerate(DILATIONS):
        blocks_per_stream = UNITS // d
        shift = int(math.log2(blocks_per_stream))

        def unit(u, carry, bi=bi, d=d, blocks_per_stream=blocks_per_stream, shift=shift):
            r = u >> shift
            n = u & (blocks_per_stream - 1)
            if d == 1:
                idx = pl.ds(pl.multiple_of(u * N_SUB, N_SUB), N_SUB)
            else:
                idx = pl.ds(r + d * N_SUB * n, N_SUB, stride=d)
            slot = _STREAM_SLOT0[bi] + r
            q = q_ref[idx, :] * (ATTN_SCALE * LOG2_E)
            kc = k_ref[idx, :].astype(BF16)
            vc = v_ref[idx, :].astype(BF16)
            kp = kprev[slot]
            vp = vprev[slot]
            kprev[slot] = kc
            vprev[slot] = vc
            zero = jnp.zeros_like(q)
            q2 = jnp.concatenate([jnp.where(head0, q, zero), jnp.where(head0, zero, q)], axis=0).astype(BF16)
            first = jnp.logical_and(t == 0, n == 0).astype(jnp.int32)
            q_aug = jnp.concatenate([q2, one_hot], axis=1)
            k_aug = jnp.concatenate([jnp.concatenate([kp, kc], axis=0), bias_scr[first]], axis=1)
            v_aug = jnp.concatenate([jnp.concatenate([vp, vc], axis=0), ones_blk], axis=1)
            s = lax.dot_general(q_aug, k_aug, (((1,), (1,)), ((), ())), preferred_element_type=F32)
            m = jnp.max(s, axis=-1, keepdims=True)
            p = jnp.exp2(s - m)
            pv = jnp.dot(p.astype(BF16), v_aug, preferred_element_type=F32)
            o_un = jnp.where(head0, pv[:N_SUB, :LANES], pv[N_SUB:, :LANES])
            mb = jnp.where(head0, m[:N_SUB], m[N_SUB:])
            lb = jnp.where(head0, pv[:N_SUB, LANES:], pv[N_SUB:, LANES:])
            if bi == 0:
                m_run[idx, :] = mb
                l_run[idx, :] = lb
                acc_run[idx, :] = o_un
            else:
                m_old = m_run[idx, :]
                m_new = jnp.maximum(m_old, mb)
                a = jnp.exp2(m_old - m_new)
                b = jnp.exp2(mb - m_new)
                l_new = a * l_run[idx, :] + b * lb
                acc_new = a * acc_run[idx, :] + b * o_un
                if bi < len(DILATIONS) - 1:
                    m_run[idx, :] = m_new
                    l_run[idx, :] = l_new
                    acc_run[idx, :] = acc_new
                else:
                    o_ref[idx, :] = acc_new / l_new
            return carry

        lax.fori_loop(0, UNITS, unit, 0, unroll=ATTN_UNROLL)


def _attn_prompt(h, batch, seq):
    assert seq % ATTN_TILE == 0
    tiles = seq // ATTN_TILE
    pairs = D_A // LANES
    blk = (ATTN_TILE, LANES)
    return pl.pallas_call(
        _attn_prompt_kernel,
        grid=(batch, pairs, tiles),
        in_specs=[
            pl.BlockSpec(blk, lambda b, hp, t: (b * tiles + t, hp)),
            pl.BlockSpec(blk, lambda b, hp, t: (b * tiles + t, pairs + hp)),
            pl.BlockSpec(blk, lambda b, hp, t: (b * tiles + t, 2 * pairs + hp)),
        ],
        out_specs=pl.BlockSpec(blk, lambda b, hp, t: (b * tiles + t, hp)),
        out_shape=jax.ShapeDtypeStruct((batch * seq, D_A), F32),
        scratch_shapes=[
            pltpu.VMEM((_N_STREAM_SLOTS, N_SUB, LANES), BF16),
            pltpu.VMEM((_N_STREAM_SLOTS, N_SUB, LANES), BF16),
            pltpu.VMEM((2, 2 * N_SUB, N_SUB), BF16),
            pltpu.VMEM(blk, F32),
            pltpu.VMEM(blk, F32),
            pltpu.VMEM(blk, F32),
        ],
        compiler_params=pltpu.CompilerParams(
            dimension_semantics=("arbitrary", "arbitrary", "arbitrary"), vmem_limit_bytes=VMEM_LIMIT_BYTES),
        name="attn_prompt",
    )(h, h, h)


def _sample_key_multiplicity(w_buf, n_new):
    cnt = np.zeros((16, w_buf), np.float32)
    j = np.arange(w_buf)
    for t in range(n_new):
        dist = w_buf + t - j
        for d in DILATIONS:
            cnt[t] += ((dist % d == 0) & (dist // d >= 1) & (dist // d <= N_SUB)).astype(np.float32)
    return cnt


def _sample_kv_step(q_ref, kn_ref, vn_ref, cnt_ref, kt_ref, vt_ref, o_ref, kto_ref, vto_ref):
    w_buf = kt_ref.shape[-1]
    rows = cnt_ref.shape[0]
    n_new, width = q_ref.shape
    cnt = cnt_ref[...]
    valid = cnt > 0.0
    pad = jnp.zeros((rows - n_new, width), F32)
    q16 = jnp.concatenate([q_ref[...] * ATTN_SCALE, pad], axis=0)
    kn16 = jnp.concatenate([kn_ref[...], pad], axis=0)
    vn16 = jnp.concatenate([vn_ref[...], pad], axis=0)
    tpad = jnp.zeros((LANES - rows, width), F32)
    kn_t = jnp.concatenate([kn16, tpad], axis=0).T
    vn_t = jnp.concatenate([vn16, tpad], axis=0).T
    rr = lax.broadcasted_iota(jnp.int32, (rows, rows), 0)
    cc = lax.broadcasted_iota(jnp.int32, (rows, rows), 1)
    cnt_new = jnp.where(cc == rr, float(len(DILATIONS)), jnp.where(cc < rr, 1.0, 0.0))
    valid_new = cnt_new > 0.0
    outs = []
    for h in range(HEADS_PER_KV_STEP):
        cols = slice(h * HEAD_DIM, (h + 1) * HEAD_DIM)
        kth = kt_ref[h]
        vth = vt_ref[h]
        qh = q16[:, cols].astype(BF16)
        knh = kn16[:, cols].astype(BF16)
        vnh = vn16[:, cols].astype(BF16)
        s = jnp.dot(qh, kth.astype(BF16), preferred_element_type=F32)
        s_new = lax.dot_general(qh, knh, (((1,), (1,)), ((), ())), preferred_element_type=F32)
        s = jnp.where(valid, s, NEG_INF)
        s_new = jnp.where(valid_new, s_new, NEG_INF)
        m = jnp.maximum(jnp.max(s, axis=-1, keepdims=True), jnp.max(s_new, axis=-1, keepdims=True))
        p = jnp.exp(s - m) * cnt
        p_new = jnp.exp(s_new - m) * cnt_new
        l = jnp.sum(p, axis=-1, keepdims=True) + jnp.sum(p_new, axis=-1, keepdims=True)
        pv = lax.dot_general(p.astype(BF16), vth.astype(BF16), (((1,), (1,)), ((), ())),
                             preferred_element_type=F32)
        pv = pv + jnp.dot(p_new.astype(BF16), vnh, preferred_element_type=F32)
        outs.append((pv / l)[:n_new])
        for src, new_t, dst in ((kth, kn_t, kto_ref), (vth, vn_t, vto_ref)):
            dst[h, :, :w_buf - n_new] = src[:, n_new:]
            dst[h, :, w_buf - n_new:] = new_t[cols, :n_new]
    o_ref[...] = jnp.concatenate(outs, axis=-1)


def _mix_half_step(x_ref, oa_ref, u_ref, v_ref, sg_ref, sbias_ref, vg_ref, vb_ref, ga_ref, gb_ref,
                   wout_ref, g2_ref, b2_ref, alpha, sample):
    vn = _layer_norm(v_ref[...], vg_ref[...], vb_ref[...])
    n_chunks = vn.shape[0] // CHUNK
    if sample:
        blocks = []
        for ti in range(n_chunks):
            acc = sbias_ref[ti:ti + 1, :]
            for j in range(ti + 1):
                acc = acc + sg_ref[ti * n_chunks + j:ti * n_chunks + j + 1, :] * vn[j * CHUNK:(j + 1) * CHUNK, :]
            blocks.append(acc)
        mix = jnp.concatenate(blocks, axis=0)
    else:
        ri = lax.broadcasted_iota(jnp.int32, (CHUNK, CHUNK), 0)
        ci = lax.broadcasted_iota(jnp.int32, (CHUNK, CHUNK), 1)
        causal = ci <= ri
        vnb = vn.astype(BF16)
        per_group = []
        for g in range(N_GROUPS_B):
            w = jnp.where(causal, sg_ref[g], 0.0).astype(BF16)
            lanes = slice(g * GROUP_B, (g + 1) * GROUP_B)
            rhs = jnp.concatenate([vnb[c * CHUNK:(c + 1) * CHUNK, lanes] for c in range(n_chunks)], axis=1)
            per_group.append(jnp.dot(w, rhs, preferred_element_type=F32))
        mix = jnp.concatenate(
            [jnp.concatenate([per_group[g][:, c * CHUNK:(c + 1) * CHUNK] for g in range(N_GROUPS_B)], axis=1)
             + sbias_ref[...] for c in range(n_chunks)], axis=0)
    o_b = u_ref[...] * mix
    cat = jnp.concatenate([_rms_norm(oa_ref[...], ga_ref[...]), _rms_norm(o_b, gb_ref[...])], axis=-1)
    mixed = jnp.dot(cat.astype(BF16), wout_ref[...], preferred_element_type=F32)
    return _layer_norm(alpha * x_ref[...] + mixed, g2_ref[...], b2_ref[...]), vn


def _mix_ffn_sample_kernel(*refs, alpha):
    mix_refs, (win2_ref, wout2_ref, g3_ref, b3_ref, y_ref, vn_ref) = refs[:13], refs[13:]
    x2, vn = _mix_half_step(*mix_refs, alpha, sample=True)
    vn_ref[...] = vn
    y_ref[...] = _ffn_half_step(x2, win2_ref, wout2_ref, g3_ref, b3_ref, alpha)


def _mix_ffn_sample(x1, o_a, h, mix_consts, ffn_consts, alpha):
    n = x1.shape[0]
    u_blk = 3 * D_A // D_B
    row = lambda width, j: pl.BlockSpec((n, width), lambda i, j=j: (0, j))
    consts = list(mix_consts) + list(ffn_consts)
    return pl.pallas_call(
        functools.partial(_mix_ffn_sample_kernel, alpha=alpha),
        grid=(1,),
        in_specs=[row(D_MODEL, 0), row(D_A, 0), row(D_B, u_blk), row(D_B, u_blk + 1)]
        + [_const_spec(c.shape) for c in consts],
        out_specs=[row(D_MODEL, 0), row(D_B, 0)],
        out_shape=[jax.ShapeDtypeStruct((n, D_MODEL), F32), jax.ShapeDtypeStruct((n, D_B), F32)],
        compiler_params=pltpu.CompilerParams(
            dimension_semantics=("arbitrary",), vmem_limit_bytes=VMEM_LIMIT_BYTES),
        name="mix_ffn_sample",
    )(x1, o_a, h, h, *consts)


def _prompt_tail_kernel(*refs, alpha):
    mix_refs = refs[:13]
    win2_ref, wout2_ref, g3_ref, b3_ref = refs[13:17]
    kv_in = refs[17:23]
    y_ref = refs[23]
    kv_out = refs[24:27]
    x2_scr = refs[27]

    @pl.when(pl.program_id(0) == 0)
    def _():
        x2_scr[...] = jnp.zeros_like(x2_scr)

    x2_prev = x2_scr[...]
    x2_next, _ = _mix_half_step(*mix_refs, alpha, sample=False)
    y_ref[...] = _ffn_half_step(x2_prev, win2_ref, wout2_ref, g3_ref, b3_ref, alpha)
    x2_scr[...] = x2_next
    _sample_kv_step(*kv_in, *kv_out)


def _prompt_tail(x1, o_a, h, mix_consts, ffn_consts, h_b, kt, vt, alpha):
    n = x1.shape[0]
    assert n % TAIL_TILE == 0
    n_tiles = n // TAIL_TILE
    dec_batch, n_new, _ = h_b.shape
    w_buf = kt.shape[-1]
    groups = N_HEADS // HEADS_PER_KV_STEP
    width = HEADS_PER_KV_STEP * HEAD_DIM
    n_side = dec_batch * groups
    steps = max(n_tiles + 1, n_side)
    cnt = jnp.asarray(_sample_key_multiplicity(w_buf, n_new))
    u_blk = 3 * D_A // D_B

    tile_in = lambda i: jnp.minimum(i, n_tiles - 1)
    row = lambda w, j: pl.BlockSpec((TAIL_TILE, w), lambda i, j=j: (tile_in(i), j))
    side = lambda i: jnp.minimum(i, n_side - 1)
    new_spec = lambda off: pl.BlockSpec(
        (None, n_new, width), lambda i, off=off: (side(i) // groups, 0, off + side(i) % groups))
    kv_spec = pl.BlockSpec((None, HEADS_PER_KV_STEP, HEAD_DIM, w_buf),
                           lambda i: (side(i) // groups, side(i) % groups, 0, 0))
    consts = list(mix_consts) + list(ffn_consts)
    return pl.pallas_call(
        functools.partial(_prompt_tail_kernel, alpha=alpha),
        grid=(steps,),
        in_specs=[row(D_MODEL, 0), row(D_A, 0), row(D_B, u_blk), row(D_B, u_blk + 1)]
        + [_const_spec(c.shape) for c in consts]
        + [new_spec(0), new_spec(groups), new_spec(2 * groups), _const_spec(cnt.shape), kv_spec, kv_spec],
        out_specs=[pl.BlockSpec((TAIL_TILE, D_MODEL), lambda i: (jnp.clip(i - 1, 0, n_tiles - 1), 0)),
                   new_spec(0), kv_spec, kv_spec],
        out_shape=[
            jax.ShapeDtypeStruct((n, D_MODEL), F32),
            jax.ShapeDtypeStruct((dec_batch, n_new, D_A), F32),
            jax.ShapeDtypeStruct(kt.shape, F32),
            jax.ShapeDtypeStruct(vt.shape, F32),
        ],
        scratch_shapes=[pltpu.VMEM((TAIL_TILE, D_MODEL), F32)],
        compiler_params=pltpu.CompilerParams(
            dimension_semantics=("arbitrary",), vmem_limit_bytes=VMEM_LIMIT_BYTES),
        name="prompt_tail",
    )(x1, o_a, h, h, *consts, h_b, h_b, h_b, cnt, kt, vt)


def kernel(x_prompt, x_sample, cache_k, cache_v, ffn1_w_in, ffn1_w_out, ln1_g, ln1_b, w_in, sgu_w, sgu_b,
           sgu_v_g, sgu_v_b, out_a_g, out_b_g, w_out, ln2_g, ln2_b, ffn2_w_in, ffn2_w_out, ln3_g, ln3_b):
    depth = ffn1_w_in.shape[0]
    batch, seq, _ = x_prompt.shape
    dec_batch, n_new, _ = x_sample.shape
    w_buf = cache_k.shape[2]
    assert TOKEN_TILE == n_new * CHUNK and dec_batch == CHUNK and seq % ATTN_TILE == 0 and w_buf == W_MAX
    alpha = (2.0 * depth) ** 0.25
    keep = min(W_MAX, seq)

    xp = x_prompt.reshape(batch * seq, D_MODEL)
    xs = jnp.transpose(x_sample, (1, 0, 2)).reshape(n_new * dec_batch, D_MODEL)
    kp_l, vp_l, ks_l, vs_l, us_l = [], [], [], [], []
    tril = jnp.tril(jnp.ones((CHUNK, CHUNK), bool))
    for layer in range(depth):
        w1i, w1o = ffn1_w_in[layer].astype(BF16), ffn1_w_out[layer].astype(BF16)
        w2i, w2o = ffn2_w_in[layer].astype(BF16), ffn2_w_out[layer].astype(BF16)
        wp, wo = w_in[layer].astype(BF16), w_out[layer].astype(BF16)
        row = lambda a: a[layer][None, :]
        g1, b1, g2, b2, g3, b3 = row(ln1_g), row(ln1_b), row(ln2_g), row(ln2_b), row(ln3_g), row(ln3_b)
        vg, vb, ga, gb = row(sgu_v_g), row(sgu_v_b), row(out_a_g), row(out_b_g)
        sgw = sgu_w[layer]
        sg_bias = jnp.repeat(sgu_b[layer].T, GROUP_B, axis=1)
        corner = jnp.where(tril[:n_new, :n_new], sgw[:, :n_new, :n_new], 0.0)
        coef = jnp.repeat(jnp.transpose(corner, (1, 2, 0)).reshape(n_new * n_new, N_GROUPS_B), GROUP_B, axis=1)

        ffn2_consts = (w2i, w2o, g3, b3)
        mix_tail = (vg, vb, ga, gb, wo, g2, b2)

        x1, h = _ffn_proj_block(xp, w1i, w1o, g1, b1, wp, alpha)
        x1s, hs = _ffn_proj_block(xs, w1i, w1o, g1, b1, wp, alpha)
        hs_b = jnp.transpose(hs.reshape(n_new, dec_batch, -1), (1, 0, 2))
        kt = jnp.transpose(cache_k[layer], (0, 2, 3, 1))
        vt = jnp.transpose(cache_v[layer], (0, 2, 3, 1))

        o_a = _attn_prompt(h, batch, seq)
        xp, o_s, kt_new, vt_new = _prompt_tail(x1, o_a, h, (sgw, sg_bias) + mix_tail, ffn2_consts, hs_b, kt, vt, alpha)
        h3 = h.reshape(batch, seq, -1)
        kp_l.append(h3[:, seq - keep:, D_A:2 * D_A].reshape(batch, keep, N_HEADS, HEAD_DIM))
        vp_l.append(h3[:, seq - keep:, 2 * D_A:3 * D_A].reshape(batch, keep, N_HEADS, HEAD_DIM))

        o_s = jnp.transpose(o_s, (1, 0, 2)).reshape(n_new * dec_batch, D_A)
        xs, vn_s = _mix_ffn_sample(x1s, o_s, hs, (coef, sg_bias[:n_new]) + mix_tail, ffn2_consts, alpha)
        ks_l.append(jnp.transpose(kt_new, (0, 3, 1, 2)))
        vs_l.append(jnp.transpose(vt_new, (0, 3, 1, 2)))
        us_l.append(jnp.transpose(vn_s.reshape(n_new, dec_batch, D_B), (1, 0, 2)))

    y_p = xp.reshape(batch, seq, D_MODEL)
    y_s = jnp.transpose(xs.reshape(n_new, dec_batch, D_MODEL), (1, 0, 2))
    return (y_p, y_s, jnp.stack(kp_l), jnp.stack(vp_l), jnp.stack(ks_l), jnp.stack(vs_l), jnp.stack(us_l))
```

```python
import functools
import math

import numpy as np
import jax
import jax.numpy as jnp
from jax import lax
from jax.experimental import pallas as pl
from jax.experimental.pallas import tpu as pltpu

F32 = jnp.float32
BF16 = jnp.bfloat16

D_MODEL = 1024
HEAD_DIM = 64
N_HEADS = 8
D_A = N_HEADS * HEAD_DIM
D_B = D_MODEL - D_A
D_FF = 2816
CHUNK = 128
GROUP_B = 128
N_GROUPS_B = D_B // GROUP_B
N_SUB = 128
DILATIONS = (1, 4, 16)
W_MAX = 2048
ATTN_SCALE = HEAD_DIM ** -0.5
LOG2_E = math.log2(math.e)
LN_EPS = 1e-5
NEG_INF = -1e30
FFN_HALF = 0.5

LANES = 128
VMEM_LIMIT_BYTES = 56 * 1024 * 1024

TOKEN_TILE = 512
TAIL_TILE = 256
FF_CHUNKS = ((0, 1536), (1536, 1280))
ATTN_TILE = 2048
UNITS = ATTN_TILE // N_SUB
ATTN_UNROLL = 16
HEADS_PER_KV_STEP = 4
KV_ROWS = 8


def _const_spec(shape):
    nd = len(shape)
    return pl.BlockSpec(shape, lambda *_: (0,) * nd, pipeline_mode=pl.Buffered(1))


def _layer_norm(x, g, b):
    mu = jnp.mean(x, axis=-1, keepdims=True)
    xc = x - mu
    var = jnp.mean(xc * xc, axis=-1, keepdims=True)
    return xc * lax.rsqrt(var + LN_EPS) * g + b


def _rms_norm(x, g):
    return x * lax.rsqrt(jnp.mean(x * x, axis=-1, keepdims=True) + LN_EPS) * g


def _ffn_half_step(x, win_ref, wout_ref, g_ref, b_ref, alpha):
    xb = x.astype(BF16)
    acc = None
    for c0, cw in FF_CHUNKS:
        gate = jnp.dot(xb, win_ref[:, c0:c0 + cw], preferred_element_type=F32)
        up = jnp.dot(xb, win_ref[:, D_FF + c0:D_FF + c0 + cw], preferred_element_type=F32)
        act = (gate * (1.0 / (1.0 + jnp.exp(-gate))) * up).astype(BF16)
        part = jnp.dot(act, wout_ref[c0:c0 + cw, :], preferred_element_type=F32)
        acc = part if acc is None else acc + part
    return _layer_norm(alpha * x + FFN_HALF * acc, g_ref[...], b_ref[...])


def _ffn_proj_kernel(x_ref, win_ref, wout_ref, g_ref, b_ref, wp_ref, y_ref, h_ref, *, alpha):
    y = _ffn_half_step(x_ref[...], win_ref, wout_ref, g_ref, b_ref, alpha)
    y_ref[...] = y
    h_ref[...] = jnp.dot(y.astype(BF16), wp_ref[...], preferred_element_type=F32)


def _ffn_proj_block(x, w_in, w_out, g, b, w_proj, alpha):
    n = x.shape[0]
    assert n % TOKEN_TILE == 0
    d_h = w_proj.shape[1]
    row_spec = pl.BlockSpec((TOKEN_TILE, D_MODEL), lambda i: (i, 0))
    consts = [w_in, w_out, g, b, w_proj]
    return pl.pallas_call(
        functools.partial(_ffn_proj_kernel, alpha=alpha),
        grid=(n // TOKEN_TILE,),
        in_specs=[row_spec] + [_const_spec(c.shape) for c in consts],
        out_specs=[row_spec, pl.BlockSpec((TOKEN_TILE, d_h), lambda i: (i, 0))],
        out_shape=[jax.ShapeDtypeStruct((n, D_MODEL), F32), jax.ShapeDtypeStruct((n, d_h), F32)],
        compiler_params=pltpu.CompilerParams(
            dimension_semantics=("arbitrary",), vmem_limit_bytes=VMEM_LIMIT_BYTES),
        name="ffn_proj_block",
    )(x, *consts)


_STREAM_SLOT0 = (0, 1, 5)
_N_STREAM_SLOTS = 21


def _attn_prompt_kernel(q_ref, k_ref, v_ref, o_ref, kt_ref, vt_ref, kprev, vprev, bias_scr, m_br, l_br, acc_br):
    t = pl.program_id(2)

    @pl.when(t == 0)
    def _():
        kprev[...] = jnp.zeros_like(kprev)
        vprev[...] = jnp.zeros_like(vprev)

    @pl.when(t == pl.num_programs(2) - 1)
    def _():
        kt_ref[...] = k_ref[...].T.reshape(kt_ref.shape)
        vt_ref[...] = v_ref[...].T.reshape(vt_ref.shape)

    lane = lax.broadcasted_iota(jnp.int32, (N_SUB, LANES), 1)
    head0 = lane < HEAD_DIM
    key = lax.broadcasted_iota(jnp.int32, (2 * N_SUB, N_SUB), 0)
    qry = lax.broadcasted_iota(jnp.int32, (2 * N_SUB, N_SUB), 1)
    is_prev = key < N_SUB
    valid = jnp.logical_or(jnp.logical_and(is_prev, key >= qry),
                           jnp.logical_and(jnp.logical_not(is_prev), (key - N_SUB) <= qry))
    neg = jnp.full((2 * N_SUB, N_SUB), NEG_INF, F32)
    zeros = jnp.zeros((2 * N_SUB, N_SUB), F32)
    bias_scr[0] = jnp.where(valid, zeros, neg).astype(BF16)
    bias_scr[1] = jnp.where(jnp.logical_and(valid, jnp.logical_not(is_prev)), zeros, neg).astype(BF16)
    one_hot = ((key & (N_SUB - 1)) == qry).astype(F32).astype(BF16)
    ones_blk = jnp.ones((2 * N_SUB, LANES), BF16)

    for bi, d in enumerate(DILATIONS):
        blocks_per_stream = UNITS // d
        shift = int(math.log2(blocks_per_stream))

        def unit(u, carry, bi=bi, d=d, blocks_per_stream=blocks_per_stream, shift=shift):
            r = u >> shift
            n = u & (blocks_per_stream - 1)
            if d == 1:
                idx = pl.ds(pl.multiple_of(u * N_SUB, N_SUB), N_SUB)
            else:
                idx = pl.ds(r + d * N_SUB * n, N_SUB, stride=d)
            slot = _STREAM_SLOT0[bi] + r
            q = q_ref[idx, :] * (ATTN_SCALE * LOG2_E)
            kc = k_ref[idx, :].astype(BF16)
            vc = v_ref[idx, :].astype(BF16)
            kp = kprev[slot]
            vp = vprev[slot]
            kprev[slot] = kc
            vprev[slot] = vc
            zero = jnp.zeros_like(q)
            q2 = jnp.concatenate([jnp.where(head0, q, zero), jnp.where(head0, zero, q)], axis=0).astype(BF16)
            first = jnp.logical_and(t == 0, n == 0).astype(jnp.int32)
            q_aug = jnp.concatenate([q2, one_hot], axis=1)
            k_aug = jnp.concatenate([jnp.concatenate([kp, kc], axis=0), bias_scr[first]], axis=1)
            v_aug = jnp.concatenate([jnp.concatenate([vp, vc], axis=0), ones_blk], axis=1)
            s = lax.dot_general(q_aug, k_aug, (((1,), (1,)), ((), ())), preferred_element_type=F32)
            m = jnp.max(s, axis=-1, keepdims=True)
            p = jnp.exp2(s - m)
            pv = jnp.dot(p.astype(BF16), v_aug, preferred_element_type=F32)
            o_un = jnp.where(head0, pv[:N_SUB, :LANES], pv[N_SUB:, :LANES])
            mb = jnp.where(head0, m[:N_SUB], m[N_SUB:])
            lb = jnp.where(head0, pv[:N_SUB, LANES:], pv[N_SUB:, LANES:])
            m_br[bi, idx, :] = mb
            l_br[bi, idx, :] = lb
            acc_br[bi, idx, :] = o_un
            return carry

        lax.fori_loop(0, UNITS, unit, 0, unroll=ATTN_UNROLL)

    def merge(c, carry):
        rows = pl.ds(pl.multiple_of(c * N_SUB, N_SUB), N_SUB)
        ms = [m_br[bi, rows, :] for bi in range(len(DILATIONS))]
        m_all = functools.reduce(jnp.maximum, ms)
        ws = [jnp.exp2(m - m_all) for m in ms]
        l_all = sum(w * l_br[bi, rows, :] for bi, w in enumerate(ws))
        acc_all = sum(w * acc_br[bi, rows, :] for bi, w in enumerate(ws))
        o_ref[rows, :] = acc_all / l_all
        return carry

    lax.fori_loop(0, UNITS, merge, 0, unroll=4)


def _attn_prompt(h, batch, seq):
    assert seq % ATTN_TILE == 0
    tiles = seq // ATTN_TILE
    pairs = D_A // LANES
    blk = (ATTN_TILE, LANES)
    heads_per_pair = LANES // HEAD_DIM
    state_spec = pl.BlockSpec((None, heads_per_pair, HEAD_DIM, ATTN_TILE), lambda b, hp, t: (b, hp, 0, 0))
    state_shape = jax.ShapeDtypeStruct((batch, N_HEADS, HEAD_DIM, ATTN_TILE), F32)
    return pl.pallas_call(
        _attn_prompt_kernel,
        grid=(batch, pairs, tiles),
        in_specs=[
            pl.BlockSpec(blk, lambda b, hp, t: (b * tiles + t, hp)),
            pl.BlockSpec(blk, lambda b, hp, t: (b * tiles + t, pairs + hp)),
            pl.BlockSpec(blk, lambda b, hp, t: (b * tiles + t, 2 * pairs + hp)),
        ],
        out_specs=[pl.BlockSpec(blk, lambda b, hp, t: (b * tiles + t, hp)), state_spec, state_spec],
        out_shape=[jax.ShapeDtypeStruct((batch * seq, D_A), F32), state_shape, state_shape],
        scratch_shapes=[
            pltpu.VMEM((_N_STREAM_SLOTS, N_SUB, LANES), BF16),
            pltpu.VMEM((_N_STREAM_SLOTS, N_SUB, LANES), BF16),
            pltpu.VMEM((2, 2 * N_SUB, N_SUB), BF16),
            pltpu.VMEM((len(DILATIONS),) + blk, F32),
            pltpu.VMEM((len(DILATIONS),) + blk, F32),
            pltpu.VMEM((len(DILATIONS),) + blk, F32),
        ],
        compiler_params=pltpu.CompilerParams(
            dimension_semantics=("arbitrary", "arbitrary", "arbitrary"), vmem_limit_bytes=VMEM_LIMIT_BYTES),
        name="attn_prompt",
    )(h, h, h)


def _sample_key_multiplicity(w_buf, n_new):
    cnt = np.zeros((KV_ROWS, w_buf), np.float32)
    j = np.arange(w_buf)
    for t in range(n_new):
        dist = w_buf + t - j
        for d in DILATIONS:
            cnt[t] += ((dist % d == 0) & (dist // d >= 1) & (dist // d <= N_SUB)).astype(np.float32)
    return cnt


def _sample_kv_step(q_ref, kn_ref, vn_ref, cnt_ref, kt_ref, vt_ref, o_ref, kto_ref, vto_ref):
    w_buf = kt_ref.shape[-1]
    rows = cnt_ref.shape[0]
    n_new, width = q_ref.shape
    cnt = cnt_ref[...]
    valid = cnt > 0.0
    pad = jnp.zeros((rows - n_new, width), F32)
    q16 = jnp.concatenate([q_ref[...] * ATTN_SCALE, pad], axis=0)
    kn16 = jnp.concatenate([kn_ref[...], pad], axis=0)
    vn16 = jnp.concatenate([vn_ref[...], pad], axis=0)
    tpad = jnp.zeros((LANES - rows, width), F32)
    kn_t = jnp.concatenate([kn16, tpad], axis=0).T
    vn_t = jnp.concatenate([vn16, tpad], axis=0).T
    rr = lax.broadcasted_iota(jnp.int32, (rows, rows), 0)
    cc = lax.broadcasted_iota(jnp.int32, (rows, rows), 1)
    cnt_new = jnp.where(cc == rr, float(len(DILATIONS)), jnp.where(cc < rr, 1.0, 0.0))
    valid_new = cnt_new > 0.0
    outs = []
    for h in range(HEADS_PER_KV_STEP):
        cols = slice(h * HEAD_DIM, (h + 1) * HEAD_DIM)
        kth = kt_ref[h]
        vth = vt_ref[h]
        qh = q16[:, cols].astype(BF16)
        knh = kn16[:, cols].astype(BF16)
        vnh = vn16[:, cols].astype(BF16)
        s = jnp.dot(qh, kth.astype(BF16), preferred_element_type=F32)
        s_new = lax.dot_general(qh, knh, (((1,), (1,)), ((), ())), preferred_element_type=F32)
        s = jnp.where(valid, s, NEG_INF)
        s_new = jnp.where(valid_new, s_new, NEG_INF)
        m = jnp.maximum(jnp.max(s, axis=-1, keepdims=True), jnp.max(s_new, axis=-1, keepdims=True))
        p = jnp.exp(s - m) * cnt
        p_new = jnp.exp(s_new - m) * cnt_new
        l = jnp.sum(p, axis=-1, keepdims=True) + jnp.sum(p_new, axis=-1, keepdims=True)
        pv = lax.dot_general(p.astype(BF16), vth.astype(BF16), (((1,), (1,)), ((), ())),
                             preferred_element_type=F32)
        pv = pv + jnp.dot(p_new.astype(BF16), vnh, preferred_element_type=F32)
        outs.append((pv / l)[:n_new])
        for src, new_t, dst in ((kth, kn_t, kto_ref), (vth, vn_t, vto_ref)):
            dst[h, :, :w_buf - n_new] = src[:, n_new:]
            dst[h, :, w_buf - n_new:] = new_t[cols, :n_new]
    o_ref[...] = jnp.concatenate(outs, axis=-1)


def _mix_half_step(x_ref, oa_ref, u_ref, v_ref, sg_ref, sbias_ref, vg_ref, vb_ref, ga_ref, gb_ref,
                   wout_ref, g2_ref, b2_ref, alpha, sample):
    vn = _layer_norm(v_ref[...], vg_ref[...], vb_ref[...])
    n_chunks = vn.shape[0] // CHUNK
    if sample:
        blocks = []
        for ti in range(n_chunks):
            acc = sbias_ref[ti:ti + 1, :]
            for j in range(ti + 1):
                acc = acc + sg_ref[ti * n_chunks + j:ti * n_chunks + j + 1, :] * vn[j * CHUNK:(j + 1) * CHUNK, :]
            blocks.append(acc)
        mix = jnp.concatenate(blocks, axis=0)
    else:
        ri = lax.broadcasted_iota(jnp.int32, (CHUNK, CHUNK), 0)
        ci = lax.broadcasted_iota(jnp.int32, (CHUNK, CHUNK), 1)
        causal = ci <= ri
        vnb = vn.astype(BF16)
        per_group = []
        for g in range(N_GROUPS_B):
            w = jnp.where(causal, sg_ref[g], 0.0).astype(BF16)
            lanes = slice(g * GROUP_B, (g + 1) * GROUP_B)
            rhs = jnp.concatenate([vnb[c * CHUNK:(c + 1) * CHUNK, lanes] for c in range(n_chunks)], axis=1)
            per_group.append(jnp.dot(w, rhs, preferred_element_type=F32))
        mix = jnp.concatenate(
            [jnp.concatenate([per_group[g][:, c * CHUNK:(c + 1) * CHUNK] for g in range(N_GROUPS_B)], axis=1)
             + sbias_ref[...] for c in range(n_chunks)], axis=0)
    o_b = u_ref[...] * mix
    cat = jnp.concatenate([_rms_norm(oa_ref[...], ga_ref[...]), _rms_norm(o_b, gb_ref[...])], axis=-1)
    mixed = jnp.dot(cat.astype(BF16), wout_ref[...], preferred_element_type=F32)
    return _layer_norm(alpha * x_ref[...] + mixed, g2_ref[...], b2_ref[...]), vn


def _mix_ffn_sample_kernel(*refs, alpha):
    mix_refs, (win2_ref, wout2_ref, g3_ref, b3_ref, y_ref, vn_ref) = refs[:13], refs[13:]
    x2, vn = _mix_half_step(*mix_refs, alpha, sample=True)
    vn_ref[...] = vn
    y_ref[...] = _ffn_half_step(x2, win2_ref, wout2_ref, g3_ref, b3_ref, alpha)


def _mix_ffn_sample(x1, o_a, h, mix_consts, ffn_consts, alpha):
    n = x1.shape[0]
    u_blk = 3 * D_A // D_B
    row = lambda width, j: pl.BlockSpec((n, width), lambda i, j=j: (0, j))
    consts = list(mix_consts) + list(ffn_consts)
    return pl.pallas_call(
        functools.partial(_mix_ffn_sample_kernel, alpha=alpha),
        grid=(1,),
        in_specs=[row(D_MODEL, 0), row(D_A, 0), row(D_B, u_blk), row(D_B, u_blk + 1)]
        + [_const_spec(c.shape) for c in consts],
        out_specs=[row(D_MODEL, 0), row(D_B, 0)],
        out_shape=[jax.ShapeDtypeStruct((n, D_MODEL), F32), jax.ShapeDtypeStruct((n, D_B), F32)],
        compiler_params=pltpu.CompilerParams(
            dimension_semantics=("arbitrary",), vmem_limit_bytes=VMEM_LIMIT_BYTES),
        name="mix_ffn_sample",
    )(x1, o_a, h, h, *consts)


def _prompt_tail_kernel(*refs, alpha):
    mix_refs = refs[:13]
    win2_ref, wout2_ref, g3_ref, b3_ref = refs[13:17]
    kv_in = refs[17:23]
    y_ref = refs[23]
    kv_out = refs[24:27]
    x2_scr = refs[27]

    @pl.when(pl.program_id(0) == 0)
    def _():
        x2_scr[...] = jnp.zeros_like(x2_scr)

    x2_prev = x2_scr[...]
    x2_next, _ = _mix_half_step(*mix_refs, alpha, sample=False)
    y_ref[...] = _ffn_half_step(x2_prev, win2_ref, wout2_ref, g3_ref, b3_ref, alpha)
    x2_scr[...] = x2_next
    _sample_kv_step(*kv_in, *kv_out)


def _prompt_tail(x1, o_a, h, mix_consts, ffn_consts, h_b, kt, vt, alpha):
    n = x1.shape[0]
    assert n % TAIL_TILE == 0
    n_tiles = n // TAIL_TILE
    dec_batch, n_new, _ = h_b.shape
    w_buf = kt.shape[-1]
    groups = N_HEADS // HEADS_PER_KV_STEP
    width = HEADS_PER_KV_STEP * HEAD_DIM
    n_side = dec_batch * groups
    steps = max(n_tiles + 1, n_side)
    cnt = jnp.asarray(_sample_key_multiplicity(w_buf, n_new))
    u_blk = 3 * D_A // D_B

    tile_in = lambda i: jnp.minimum(i, n_tiles - 1)
    row = lambda w, j: pl.BlockSpec((TAIL_TILE, w), lambda i, j=j: (tile_in(i), j))
    side = lambda i: jnp.minimum(i, n_side - 1)
    new_spec = lambda off: pl.BlockSpec(
        (None, n_new, width), lambda i, off=off: (side(i) // groups, 0, off + side(i) % groups))
    kv_spec = pl.BlockSpec((None, HEADS_PER_KV_STEP, HEAD_DIM, w_buf),
                           lambda i: (side(i) // groups, side(i) % groups, 0, 0))
    consts = list(mix_consts) + list(ffn_consts)
    return pl.pallas_call(
        functools.partial(_prompt_tail_kernel, alpha=alpha),
        grid=(steps,),
        in_specs=[row(D_MODEL, 0), row(D_A, 0), row(D_B, u_blk), row(D_B, u_blk + 1)]
        + [_const_spec(c.shape) for c in consts]
        + [new_spec(0), new_spec(groups), new_spec(2 * groups), _const_spec(cnt.shape), kv_spec, kv_spec],
        out_specs=[pl.BlockSpec((TAIL_TILE, D_MODEL), lambda i: (jnp.clip(i - 1, 0, n_tiles - 1), 0)),
                   new_spec(0), kv_spec, kv_spec],
        out_shape=[
            jax.ShapeDtypeStruct((n, D_MODEL), F32),
            jax.ShapeDtypeStruct((dec_batch, n_new, D_A), F32),
            jax.ShapeDtypeStruct(kt.shape, F32),
            jax.ShapeDtypeStruct(vt.shape, F32),
        ],
        scratch_shapes=[pltpu.VMEM((TAIL_TILE, D_MODEL), F32)],
        compiler_params=pltpu.CompilerParams(
            dimension_semantics=("arbitrary",), vmem_limit_bytes=VMEM_LIMIT_BYTES),
        name="prompt_tail",
    )(x1, o_a, h, h, *consts, h_b, h_b, h_b, cnt, kt, vt)


def kernel(x_prompt, x_sample, cache_k, cache_v, ffn1_w_in, ffn1_w_out, ln1_g, ln1_b, w_in, sgu_w, sgu_b,
           sgu_v_g, sgu_v_b, out_a_g, out_b_g, w_out, ln2_g, ln2_b, ffn2_w_in, ffn2_w_out, ln3_g, ln3_b):
    depth = ffn1_w_in.shape[0]
    batch, seq, _ = x_prompt.shape
    dec_batch, n_new, _ = x_sample.shape
    w_buf = cache_k.shape[2]
    assert dec_batch == CHUNK and seq % ATTN_TILE == 0 and w_buf == W_MAX
    alpha = (2.0 * depth) ** 0.25
    assert ATTN_TILE == min(W_MAX, seq)

    xp = x_prompt.reshape(batch * seq, D_MODEL)
    xs = jnp.transpose(x_sample, (1, 0, 2)).reshape(n_new * dec_batch, D_MODEL)
    kp_l, vp_l, ks_l, vs_l, us_l = [], [], [], [], []
    tril = jnp.tril(jnp.ones((CHUNK, CHUNK), bool))
    for layer in range(depth):
        w1i, w1o = ffn1_w_in[layer].astype(BF16), ffn1_w_out[layer].astype(BF16)
        w2i, w2o = ffn2_w_in[layer].astype(BF16), ffn2_w_out[layer].astype(BF16)
        wp, wo = w_in[layer].astype(BF16), w_out[layer].astype(BF16)
        row = lambda a: a[layer][None, :]
        g1, b1, g2, b2, g3, b3 = row(ln1_g), row(ln1_b), row(ln2_g), row(ln2_b), row(ln3_g), row(ln3_b)
        vg, vb, ga, gb = row(sgu_v_g), row(sgu_v_b), row(out_a_g), row(out_b_g)
        sgw = sgu_w[layer]
        sg_bias = jnp.repeat(sgu_b[layer].T, GROUP_B, axis=1)
        corner = jnp.where(tril[:n_new, :n_new], sgw[:, :n_new, :n_new], 0.0)
        coef = jnp.repeat(jnp.transpose(corner, (1, 2, 0)).reshape(n_new * n_new, N_GROUPS_B), GROUP_B, axis=1)

        ffn2_consts = (w2i, w2o, g3, b3)
        mix_tail = (vg, vb, ga, gb, wo, g2, b2)

        x1, h = _ffn_proj_block(xp, w1i, w1o, g1, b1, wp, alpha)
        x1s, hs = _ffn_proj_block(xs, w1i, w1o, g1, b1, wp, alpha)
        hs_b = jnp.transpose(hs.reshape(n_new, dec_batch, -1), (1, 0, 2))
        kt = jnp.transpose(cache_k[layer], (0, 2, 3, 1))
        vt = jnp.transpose(cache_v[layer], (0, 2, 3, 1))

        o_a, kt_p, vt_p = _attn_prompt(h, batch, seq)
        xp, o_s, kt_new, vt_new = _prompt_tail(x1, o_a, h, (sgw, sg_bias) + mix_tail, ffn2_consts, hs_b, kt, vt, alpha)
        kp_l.append(jnp.transpose(kt_p, (0, 3, 1, 2)))
        vp_l.append(jnp.transpose(vt_p, (0, 3, 1, 2)))

        o_s = jnp.transpose(o_s, (1, 0, 2)).reshape(n_new * dec_batch, D_A)
        xs, vn_s = _mix_ffn_sample(x1s, o_s, hs, (coef, sg_bias[:n_new]) + mix_tail, ffn2_consts, alpha)
        ks_l.append(jnp.transpose(kt_new, (0, 3, 1, 2)))
        vs_l.append(jnp.transpose(vt_new, (0, 3, 1, 2)))
        us_l.append(jnp.transpose(vn_s.reshape(n_new, dec_batch, D_B), (1, 0, 2)))

    y_p = xp.reshape(batch, seq, D_MODEL)
    y_s = jnp.transpose(xs.reshape(n_new, dec_batch, D_MODEL), (1, 0, 2))
    return (y_p, y_s, jnp.stack(kp_l), jnp.stack(vp_l), jnp.stack(ks_l), jnp.stack(vs_l), jnp.stack(us_l))
```

```python
import functools
import math

import numpy as np
import jax
import jax.numpy as jnp
from jax import lax
from jax.experimental import pallas as pl
from jax.experimental.pallas import tpu as pltpu

F32 = jnp.float32
BF16 = jnp.bfloat16

D_MODEL = 1024
HEAD_DIM = 64
N_HEADS = 8
D_A = N_HEADS * HEAD_DIM
D_B = D_MODEL - D_A
D_FF = 2816
CHUNK = 128
GROUP_B = 128
N_GROUPS_B = D_B // GROUP_B
N_SUB = 128
DILATIONS = (1, 4, 16)
W_MAX = 2048
ATTN_SCALE = HEAD_DIM ** -0.5
LOG2_E = math.log2(math.e)
LN_EPS = 1e-5
NEG_INF = -1e30
FFN_HALF = 0.5

LANES = 128
VMEM_LIMIT_BYTES = 56 * 1024 * 1024

TOKEN_TILE = 512
TAIL_TILE = 256
FF_CHUNKS = ((0, 1536), (1536, 1280))
ATTN_TILE = 2048
UNITS = ATTN_TILE // N_SUB
ATTN_UNROLL = 16
HEADS_PER_KV_STEP = 4
KV_ROWS = 8


def _const_spec(shape):
    nd = len(shape)
    return pl.BlockSpec(shape, lambda *_: (0,) * nd, pipeline_mode=pl.Buffered(1))


def _layer_norm(x, g, b):
    mu = jnp.mean(x, axis=-1, keepdims=True)
    xc = x - mu
    var = jnp.mean(xc * xc, axis=-1, keepdims=True)
    return xc * lax.rsqrt(var + LN_EPS) * g + b


def _rms_norm(x, g):
    return x * lax.rsqrt(jnp.mean(x * x, axis=-1, keepdims=True) + LN_EPS) * g


def _ffn_half_step(x, win_ref, wout_ref, g_ref, b_ref, alpha):
    xb = x.astype(BF16)
    acc = None
    for c0, cw in FF_CHUNKS:
        gate = jnp.dot(xb, win_ref[:, c0:c0 + cw], preferred_element_type=F32)
        up = jnp.dot(xb, win_ref[:, D_FF + c0:D_FF + c0 + cw], preferred_element_type=F32)
        act = (gate * (1.0 / (1.0 + jnp.exp(-gate))) * up).astype(BF16)
        part = jnp.dot(act, wout_ref[c0:c0 + cw, :], preferred_element_type=F32)
        acc = part if acc is None else acc + part
    return _layer_norm(alpha * x + FFN_HALF * acc, g_ref[...], b_ref[...])


def _ffn_proj_kernel(x_ref, win_ref, wout_ref, g_ref, b_ref, wp_ref, y_ref, h_ref, *, alpha):
    y = _ffn_half_step(x_ref[...], win_ref, wout_ref, g_ref, b_ref, alpha)
    y_ref[...] = y
    h_ref[...] = jnp.dot(y.astype(BF16), wp_ref[...], preferred_element_type=F32)


def _ffn_proj_block(x, w_in, w_out, g, b, w_proj, alpha):
    n = x.shape[0]
    assert n % TOKEN_TILE == 0
    d_h = w_proj.shape[1]
    row_spec = pl.BlockSpec((TOKEN_TILE, D_MODEL), lambda i: (i, 0))
    consts = [w_in, w_out, g, b, w_proj]
    return pl.pallas_call(
        functools.partial(_ffn_proj_kernel, alpha=alpha),
        grid=(n // TOKEN_TILE,),
        in_specs=[row_spec] + [_const_spec(c.shape) for c in consts],
        out_specs=[row_spec, pl.BlockSpec((TOKEN_TILE, d_h), lambda i: (i, 0))],
        out_shape=[jax.ShapeDtypeStruct((n, D_MODEL), F32), jax.ShapeDtypeStruct((n, d_h), F32)],
        compiler_params=pltpu.CompilerParams(
            dimension_semantics=("arbitrary",), vmem_limit_bytes=VMEM_LIMIT_BYTES),
        name="ffn_proj_block",
    )(x, *consts)


_STREAM_SLOT0 = (0, 1, 5)
_N_STREAM_SLOTS = 21


def _attn_prompt_kernel(q_ref, k_ref, v_ref, o_ref, kt_ref, vt_ref, kprev, vprev, bias_scr, m_run, l_run, acc_run):
    t = pl.program_id(2)

    @pl.when(t == 0)
    def _():
        kprev[...] = jnp.zeros_like(kprev)
        vprev[...] = jnp.zeros_like(vprev)

    @pl.when(t == pl.num_programs(2) - 1)
    def _():
        kt_ref[...] = k_ref[...].T.reshape(kt_ref.shape)
        vt_ref[...] = v_ref[...].T.reshape(vt_ref.shape)

    lane = lax.broadcasted_iota(jnp.int32, (N_SUB, LANES), 1)
    head0 = lane < HEAD_DIM
    key = lax.broadcasted_iota(jnp.int32, (2 * N_SUB, N_SUB), 0)
    qry = lax.broadcasted_iota(jnp.int32, (2 * N_SUB, N_SUB), 1)
    is_prev = key < N_SUB
    valid = jnp.logical_or(jnp.logical_and(is_prev, key >= qry),
                           jnp.logical_and(jnp.logical_not(is_prev), (key - N_SUB) <= qry))
    neg = jnp.full((2 * N_SUB, N_SUB), NEG_INF, F32)
    zeros = jnp.zeros((2 * N_SUB, N_SUB), F32)
    bias_scr[0] = jnp.where(valid, zeros, neg).astype(BF16)
    bias_scr[1] = jnp.where(jnp.logical_and(valid, jnp.logical_not(is_prev)), zeros, neg).astype(BF16)
    one_hot = ((key & (N_SUB - 1)) == qry).astype(F32).astype(BF16)
    ones_blk = jnp.ones((2 * N_SUB, LANES), BF16)

    for bi, d in enumerate(DILATIONS):
        blocks_per_stream = UNITS // d
        shift = int(math.log2(blocks_per_stream))

        def unit(u, carry, bi=bi, d=d, blocks_per_stream=blocks_per_stream, shift=shift):
            r = u >> shift
            n = u & (blocks_per_stream - 1)
            if d == 1:
                idx = pl.ds(pl.multiple_of(u * N_SUB, N_SUB), N_SUB)
            else:
                idx = pl.ds(r + d * N_SUB * n, N_SUB, stride=d)
            slot = _STREAM_SLOT0[bi] + r
            q = q_ref[idx, :] * (ATTN_SCALE * LOG2_E)
            kc = k_ref[idx, :].astype(BF16)
            vc = v_ref[idx, :].astype(BF16)
            kp = kprev[slot]
            vp = vprev[slot]
            kprev[slot] = kc
            vprev[slot] = vc
            zero = jnp.zeros_like(q)
            q2 = jnp.concatenate([jnp.where(head0, q, zero), jnp.where(head0, zero, q)], axis=0).astype(BF16)
            first = jnp.logical_and(t == 0, n == 0).astype(jnp.int32)
            q_aug = jnp.concatenate([q2, one_hot], axis=1)
            k_aug = jnp.concatenate([jnp.concatenate([kp, kc], axis=0), bias_scr[first]], axis=1)
            v_aug = jnp.concatenate([jnp.concatenate([vp, vc], axis=0), ones_blk], axis=1)
            s = lax.dot_general(q_aug, k_aug, (((1,), (1,)), ((), ())), preferred_element_type=F32)
            m = jnp.max(s, axis=-1, keepdims=True)
            p = jnp.exp2(s - m)
            pv = jnp.dot(p.astype(BF16), v_aug, preferred_element_type=F32)
            o_un = jnp.where(head0, pv[:N_SUB, :LANES], pv[N_SUB:, :LANES])
            mb = jnp.where(head0, m[:N_SUB], m[N_SUB:])
            lb = jnp.where(head0, pv[:N_SUB, LANES:], pv[N_SUB:, LANES:])
            if bi == 0:
                m_run[idx, :] = mb
                l_run[idx, :] = lb
                acc_run[idx, :] = o_un
            else:
                m_old = m_run[idx, :]
                m_new = jnp.maximum(m_old, mb)
                a = jnp.exp2(m_old - m_new)
                b = jnp.exp2(mb - m_new)
                l_new = a * l_run[idx, :] + b * lb
                acc_new = a * acc_run[idx, :] + b * o_un
                if bi < len(DILATIONS) - 1:
                    m_run[idx, :] = m_new
                    l_run[idx, :] = l_new
                    acc_run[idx, :] = acc_new
                else:
                    o_ref[idx, :] = acc_new / l_new
            return carry

        lax.fori_loop(0, UNITS, unit, 0, unroll=ATTN_UNROLL)


def _attn_prompt(h, batch, seq):
    assert seq % ATTN_TILE == 0
    tiles = seq // ATTN_TILE
    pairs = D_A // LANES
    blk = (ATTN_TILE, LANES)
    heads_per_pair = LANES // HEAD_DIM
    state_spec = pl.BlockSpec((None, heads_per_pair, HEAD_DIM, ATTN_TILE), lambda b, hp, t: (b, hp, 0, 0))
    state_shape = jax.ShapeDtypeStruct((batch, N_HEADS, HEAD_DIM, ATTN_TILE), F32)
    return pl.pallas_call(
        _attn_prompt_kernel,
        grid=(batch, pairs, tiles),
        in_specs=[
            pl.BlockSpec(blk, lambda b, hp, t: (b * tiles + t, hp)),
            pl.BlockSpec(blk, lambda b, hp, t: (b * tiles + t, pairs + hp)),
            pl.BlockSpec(blk, lambda b, hp, t: (b * tiles + t, 2 * pairs + hp)),
        ],
        out_specs=[pl.BlockSpec(blk, lambda b, hp, t: (b * tiles + t, hp)), state_spec, state_spec],
        out_shape=[jax.ShapeDtypeStruct((batch * seq, D_A), F32), state_shape, state_shape],
        scratch_shapes=[
            pltpu.VMEM((_N_STREAM_SLOTS, N_SUB, LANES), BF16),
            pltpu.VMEM((_N_STREAM_SLOTS, N_SUB, LANES), BF16),
            pltpu.VMEM((2, 2 * N_SUB, N_SUB), BF16),
            pltpu.VMEM(blk, F32),
            pltpu.VMEM(blk, F32),
            pltpu.VMEM(blk, F32),
        ],
        compiler_params=pltpu.CompilerParams(
            dimension_semantics=("arbitrary", "arbitrary", "arbitrary"), vmem_limit_bytes=VMEM_LIMIT_BYTES),
        name="attn_prompt",
    )(h, h, h)


def _sample_key_multiplicity(w_buf, n_new):
    cnt = np.zeros((KV_ROWS, w_buf), np.float32)
    j = np.arange(w_buf)
    for t in range(n_new):
        dist = w_buf + t - j
        for d in DILATIONS:
            cnt[t] += ((dist % d == 0) & (dist // d >= 1) & (dist // d <= N_SUB)).astype(np.float32)
    return cnt


def _sample_kv_step(q_ref, kn_ref, vn_ref, cnt_ref, kt_ref, vt_ref, o_ref, kto_ref, vto_ref):
    w_buf = kt_ref.shape[-1]
    rows = cnt_ref.shape[0]
    n_new, width = q_ref.shape
    cnt = cnt_ref[...]
    valid = cnt > 0.0
    pad = jnp.zeros((rows - n_new, width), F32)
    q16 = jnp.concatenate([q_ref[...] * ATTN_SCALE, pad], axis=0)
    kn16 = jnp.concatenate([kn_ref[...], pad], axis=0)
    vn16 = jnp.concatenate([vn_ref[...], pad], axis=0)
    tpad = jnp.zeros((LANES - rows, width), F32)
    kn_t = jnp.concatenate([kn16, tpad], axis=0).T
    vn_t = jnp.concatenate([vn16, tpad], axis=0).T
    rr = lax.broadcasted_iota(jnp.int32, (rows, rows), 0)
    cc = lax.broadcasted_iota(jnp.int32, (rows, rows), 1)
    cnt_new = jnp.where(cc == rr, float(len(DILATIONS)), jnp.where(cc < rr, 1.0, 0.0))
    valid_new = cnt_new > 0.0
    outs = []
    for h in range(HEADS_PER_KV_STEP):
        cols = slice(h * HEAD_DIM, (h + 1) * HEAD_DIM)
        kth = kt_ref[h]
        vth = vt_ref[h]
        qh = q16[:, cols].astype(BF16)
        knh = kn16[:, cols].astype(BF16)
        vnh = vn16[:, cols].astype(BF16)
        s = jnp.dot(qh, kth.astype(BF16), preferred_element_type=F32)
        s_new = lax.dot_general(qh, knh, (((1,), (1,)), ((), ())), preferred_element_type=F32)
        s = jnp.where(valid, s, NEG_INF)
        s_new = jnp.where(valid_new, s_new, NEG_INF)
        m = jnp.maximum(jnp.max(s, axis=-1, keepdims=True), jnp.max(s_new, axis=-1, keepdims=True))
        p = jnp.exp(s - m) * cnt
        p_new = jnp.exp(s_new - m) * cnt_new
        l = jnp.sum(p, axis=-1, keepdims=True) + jnp.sum(p_new, axis=-1, keepdims=True)
        pv = lax.dot_general(p.astype(BF16), vth.astype(BF16), (((1,), (1,)), ((), ())),
                             preferred_element_type=F32)
        pv = pv + jnp.dot(p_new.astype(BF16), vnh, preferred_element_type=F32)
        outs.append((pv / l)[:n_new])
        for src, new_t, dst in ((kth, kn_t, kto_ref), (vth, vn_t, vto_ref)):
            dst[h, :, :w_buf - n_new] = src[:, n_new:]
            dst[h, :, w_buf - n_new:] = new_t[cols, :n_new]
    o_ref[...] = jnp.concatenate(outs, axis=-1)


def _mix_half_step(x_ref, oa_ref, u_ref, v_ref, sg_ref, sbias_ref, vg_ref, vb_ref, ga_ref, gb_ref,
                   wout_ref, g2_ref, b2_ref, alpha, sample):
    vn = _layer_norm(v_ref[...], vg_ref[...], vb_ref[...])
    n_chunks = vn.shape[0] // CHUNK
    if sample:
        blocks = []
        for ti in range(n_chunks):
            acc = sbias_ref[ti:ti + 1, :]
            for j in range(ti + 1):
                acc = acc + sg_ref[ti * n_chunks + j:ti * n_chunks + j + 1, :] * vn[j * CHUNK:(j + 1) * CHUNK, :]
            blocks.append(acc)
        mix = jnp.concatenate(blocks, axis=0)
    else:
        ri = lax.broadcasted_iota(jnp.int32, (CHUNK, CHUNK), 0)
        ci = lax.broadcasted_iota(jnp.int32, (CHUNK, CHUNK), 1)
        causal = ci <= ri
        vnb = vn.astype(BF16)
        per_group = []
        for g in range(N_GROUPS_B):
            w = jnp.where(causal, sg_ref[g], 0.0).astype(BF16)
            lanes = slice(g * GROUP_B, (g + 1) * GROUP_B)
            rhs = jnp.concatenate([vnb[c * CHUNK:(c + 1) * CHUNK, lanes] for c in range(n_chunks)], axis=1)
            per_group.append(jnp.dot(w, rhs, preferred_element_type=F32))
        mix = jnp.concatenate(
            [jnp.concatenate([per_group[g][:, c * CHUNK:(c + 1) * CHUNK] for g in range(N_GROUPS_B)], axis=1)
             + sbias_ref[...] for c in range(n_chunks)], axis=0)
    o_b = u_ref[...] * mix
    cat = jnp.concatenate([_rms_norm(oa_ref[...], ga_ref[...]), _rms_norm(o_b, gb_ref[...])], axis=-1)
    mixed = jnp.dot(cat.astype(BF16), wout_ref[...], preferred_element_type=F32)
    return _layer_norm(alpha * x_ref[...] + mixed, g2_ref[...], b2_ref[...]), vn


def _mix_ffn_sample_kernel(*refs, alpha):
    mix_refs, (win2_ref, wout2_ref, g3_ref, b3_ref, y_ref, vn_ref) = refs[:13], refs[13:]
    x2, vn = _mix_half_step(*mix_refs, alpha, sample=True)
    vn_ref[...] = vn
    y_ref[...] = _ffn_half_step(x2, win2_ref, wout2_ref, g3_ref, b3_ref, alpha)


def _mix_ffn_sample(x1, o_a, h, mix_consts, ffn_consts, alpha):
    n = x1.shape[0]
    u_blk = 3 * D_A // D_B
    row = lambda width, j: pl.BlockSpec((n, width), lambda i, j=j: (0, j))
    consts = list(mix_consts) + list(ffn_consts)
    return pl.pallas_call(
        functools.partial(_mix_ffn_sample_kernel, alpha=alpha),
        grid=(1,),
        in_specs=[row(D_MODEL, 0), row(D_A, 0), row(D_B, u_blk), row(D_B, u_blk + 1)]
        + [_const_spec(c.shape) for c in consts],
        out_specs=[row(D_MODEL, 0), row(D_B, 0)],
        out_shape=[jax.ShapeDtypeStruct((n, D_MODEL), F32), jax.ShapeDtypeStruct((n, D_B), F32)],
        compiler_params=pltpu.CompilerParams(
            dimension_semantics=("arbitrary",), vmem_limit_bytes=VMEM_LIMIT_BYTES),
        name="mix_ffn_sample",
    )(x1, o_a, h, h, *consts)


def _prompt_tail_kernel(*refs, alpha):
    mix_refs = refs[:13]
    win2_ref, wout2_ref, g3_ref, b3_ref = refs[13:17]
    kv_in = refs[17:23]
    y_ref = refs[23]
    kv_out = refs[24:27]
    x2_scr = refs[27]

    @pl.when(pl.program_id(0) == 0)
    def _():
        x2_scr[...] = jnp.zeros_like(x2_scr)

    x2_prev = x2_scr[...]
    x2_next, _ = _mix_half_step(*mix_refs, alpha, sample=False)
    y_ref[...] = _ffn_half_step(x2_prev, win2_ref, wout2_ref, g3_ref, b3_ref, alpha)
    x2_scr[...] = x2_next
    _sample_kv_step(*kv_in, *kv_out)


def _prompt_tail(x1, o_a, h, mix_consts, ffn_consts, h_b, kt, vt, alpha):
    n = x1.shape[0]
    assert n % TAIL_TILE == 0
    n_tiles = n // TAIL_TILE
    dec_batch, n_new, _ = h_b.shape
    w_buf = kt.shape[-1]
    groups = N_HEADS // HEADS_PER_KV_STEP
    width = HEADS_PER_KV_STEP * HEAD_DIM
    n_side = dec_batch * groups
    steps = max(n_tiles + 1, n_side)
    cnt = jnp.asarray(_sample_key_multiplicity(w_buf, n_new))
    u_blk = 3 * D_A // D_B

    tile_in = lambda i: jnp.minimum(i, n_tiles - 1)
    row = lambda w, j: pl.BlockSpec((TAIL_TILE, w), lambda i, j=j: (tile_in(i), j))
    side = lambda i: jnp.minimum(i, n_side - 1)
    new_spec = lambda off: pl.BlockSpec(
        (None, n_new, width), lambda i, off=off: (side(i) // groups, 0, off + side(i) % groups))
    kv_spec = pl.BlockSpec((None, HEADS_PER_KV_STEP, HEAD_DIM, w_buf),
                           lambda i: (side(i) // groups, side(i) % groups, 0, 0))
    consts = list(mix_consts) + list(ffn_consts)
    return pl.pallas_call(
        functools.partial(_prompt_tail_kernel, alpha=alpha),
        grid=(steps,),
        in_specs=[row(D_MODEL, 0), row(D_A, 0), row(D_B, u_blk), row(D_B, u_blk + 1)]
        + [_const_spec(c.shape) for c in consts]
        + [new_spec(0), new_spec(groups), new_spec(2 * groups), _const_spec(cnt.shape), kv_spec, kv_spec],
        out_specs=[pl.BlockSpec((TAIL_TILE, D_MODEL), lambda i: (jnp.clip(i - 1, 0, n_tiles - 1), 0)),
                   new_spec(0), kv_spec, kv_spec],
        out_shape=[
            jax.ShapeDtypeStruct((n, D_MODEL), F32),
            jax.ShapeDtypeStruct((dec_batch, n_new, D_A), F32),
            jax.ShapeDtypeStruct(kt.shape, F32),
            jax.ShapeDtypeStruct(vt.shape, F32),
        ],
        scratch_shapes=[pltpu.VMEM((TAIL_TILE, D_MODEL), F32)],
        compiler_params=pltpu.CompilerParams(
            dimension_semantics=("arbitrary",), vmem_limit_bytes=VMEM_LIMIT_BYTES),
        name="prompt_tail",
    )(x1, o_a, h, h, *consts, h_b, h_b, h_b, cnt, kt, vt)


def kernel(x_prompt, x_sample, cache_k, cache_v, ffn1_w_in, ffn1_w_out, ln1_g, ln1_b, w_in, sgu_w, sgu_b,
           sgu_v_g, sgu_v_b, out_a_g, out_b_g, w_out, ln2_g, ln2_b, ffn2_w_in, ffn2_w_out, ln3_g, ln3_b):
    depth = ffn1_w_in.shape[0]
    batch, seq, _ = x_prompt.shape
    dec_batch, n_new, _ = x_sample.shape
    w_buf = cache_k.shape[2]
    assert dec_batch == CHUNK and seq % ATTN_TILE == 0 and w_buf == W_MAX
    alpha = (2.0 * depth) ** 0.25
    assert ATTN_TILE == min(W_MAX, seq)

    xp = x_prompt.reshape(batch * seq, D_MODEL)
    xs = jnp.transpose(x_sample, (1, 0, 2)).reshape(n_new * dec_batch, D_MODEL)
    kp_l, vp_l, ks_l, vs_l, us_l = [], [], [], [], []
    tril = jnp.tril(jnp.ones((CHUNK, CHUNK), bool))
    for layer in range(depth):
        w1i, w1o = ffn1_w_in[layer].astype(BF16), ffn1_w_out[layer].astype(BF16)
        w2i, w2o = ffn2_w_in[layer].astype(BF16), ffn2_w_out[layer].astype(BF16)
        wp, wo = w_in[layer].astype(BF16), w_out[layer].astype(BF16)
        row = lambda a: a[layer][None, :]
        g1, b1, g2, b2, g3, b3 = row(ln1_g), row(ln1_b), row(ln2_g), row(ln2_b), row(ln3_g), row(ln3_b)
        vg, vb, ga, gb = row(sgu_v_g), row(sgu_v_b), row(out_a_g), row(out_b_g)
        sgw = sgu_w[layer]
        sg_bias = jnp.repeat(sgu_b[layer].T, GROUP_B, axis=1)
        corner = jnp.where(tril[:n_new, :n_new], sgw[:, :n_new, :n_new], 0.0)
        coef = jnp.repeat(jnp.transpose(corner, (1, 2, 0)).reshape(n_new * n_new, N_GROUPS_B), GROUP_B, axis=1)

        ffn2_consts = (w2i, w2o, g3, b3)
        mix_tail = (vg, vb, ga, gb, wo, g2, b2)

        x1, h = _ffn_proj_block(xp, w1i, w1o, g1, b1, wp, alpha)
        x1s, hs = _ffn_proj_block(xs, w1i, w1o, g1, b1, wp, alpha)
        hs_b = jnp.transpose(hs.reshape(n_new, dec_batch, -1), (1, 0, 2))
        kt = jnp.transpose(cache_k[layer], (0, 2, 3, 1))
        vt = jnp.transpose(cache_v[layer], (0, 2, 3, 1))

        o_a, kt_p, vt_p = _attn_prompt(h, batch, seq)
        xp, o_s, kt_new, vt_new = _prompt_tail(x1, o_a, h, (sgw, sg_bias) + mix_tail, ffn2_consts, hs_b, kt, vt, alpha)
        kp_l.append(jnp.transpose(kt_p, (0, 3, 1, 2)))
        vp_l.append(jnp.transpose(vt_p, (0, 3, 1, 2)))

        o_s = jnp.transpose(o_s, (1, 0, 2)).reshape(n_new * dec_batch, D_A)
        xs, vn_s = _mix_ffn_sample(x1s, o_s, hs, (coef, sg_bias[:n_new]) + mix_tail, ffn2_consts, alpha)
        ks_l.append(jnp.transpose(kt_new, (0, 3, 1, 2)))
        vs_l.append(jnp.transpose(vt_new, (0, 3, 1, 2)))
        us_l.append(jnp.transpose(vn_s.reshape(n_new, dec_batch, D_B), (1, 0, 2)))

    y_p = xp.reshape(batch, seq, D_MODEL)
    y_s = jnp.transpose(xs.reshape(n_new, dec_batch, D_MODEL), (1, 0, 2))
    return (y_p, y_s, jnp.stack(kp_l), jnp.stack(vp_l), jnp.stack(ks_l), jnp.stack(vs_l), jnp.stack(us_l))
```

```python
import functools
import math

import numpy as np
import jax
import jax.numpy as jnp
from jax import lax
from jax.experimental import pallas as pl
from jax.experimental.pallas import tpu as pltpu

F32 = jnp.float32
BF16 = jnp.bfloat16

D_MODEL = 1024
HEAD_DIM = 64
N_HEADS = 8
D_A = N_HEADS * HEAD_DIM
D_B = D_MODEL - D_A
D_FF = 2816
CHUNK = 128
GROUP_B = 128
N_GROUPS_B = D_B // GROUP_B
N_SUB = 128
DILATIONS = (1, 4, 16)
W_MAX = 2048
ATTN_SCALE = HEAD_DIM ** -0.5
LOG2_E = math.log2(math.e)
LN_EPS = 1e-5
NEG_INF = -1e30
FFN_HALF = 0.5

LANES = 128
SUBLANES = 8
VMEM_LIMIT_BYTES = 56 * 1024 * 1024

TOKEN_TILE = 512
TAIL_TILE = 256
FF_CHUNKS = ((0, 1536), (1536, 1280))
ATTN_TILE = 2048
UNITS = ATTN_TILE // N_SUB
ATTN_UNROLL = 16
HEADS_PER_KV_STEP = 4
KV_ROWS = 8


def _const_spec(shape):
    nd = len(shape)
    return pl.BlockSpec(shape, lambda *_: (0,) * nd, pipeline_mode=pl.Buffered(1))


def _layer_norm(x, g, b):
    mu = jnp.mean(x, axis=-1, keepdims=True)
    xc = x - mu
    var = jnp.mean(xc * xc, axis=-1, keepdims=True)
    return xc * lax.rsqrt(var + LN_EPS) * g + b


def _rms_norm(x, g):
    return x * lax.rsqrt(jnp.mean(x * x, axis=-1, keepdims=True) + LN_EPS) * g


def _ffn_half_step(x, win_ref, wout_ref, g_ref, b_ref, alpha):
    xb = x.astype(BF16)
    acc = None
    for c0, cw in FF_CHUNKS:
        gate = jnp.dot(xb, win_ref[:, c0:c0 + cw], preferred_element_type=F32)
        up = jnp.dot(xb, win_ref[:, D_FF + c0:D_FF + c0 + cw], preferred_element_type=F32)
        act = (gate * (1.0 / (1.0 + jnp.exp(-gate))) * up).astype(BF16)
        part = jnp.dot(act, wout_ref[c0:c0 + cw, :], preferred_element_type=F32)
        acc = part if acc is None else acc + part
    return _layer_norm(alpha * x + FFN_HALF * acc, g_ref[...], b_ref[...])


def _ffn_proj_kernel(x_ref, win_ref, wout_ref, g_ref, b_ref, wp_ref, y_ref, h_ref, *, alpha):
    y = _ffn_half_step(x_ref[...], win_ref, wout_ref, g_ref, b_ref, alpha)
    y_ref[...] = y
    h_ref[...] = jnp.dot(y.astype(BF16), wp_ref[...], preferred_element_type=F32)


def _ffn_proj_block(x, w_in, w_out, g, b, w_proj, alpha):
    n = x.shape[0]
    assert n % TOKEN_TILE == 0
    d_h = w_proj.shape[1]
    row_spec = pl.BlockSpec((TOKEN_TILE, D_MODEL), lambda i: (i, 0))
    consts = [w_in, w_out, g, b, w_proj]
    return pl.pallas_call(
        functools.partial(_ffn_proj_kernel, alpha=alpha),
        grid=(n // TOKEN_TILE,),
        in_specs=[row_spec] + [_const_spec(c.shape) for c in consts],
        out_specs=[row_spec, pl.BlockSpec((TOKEN_TILE, d_h), lambda i: (i, 0))],
        out_shape=[jax.ShapeDtypeStruct((n, D_MODEL), F32), jax.ShapeDtypeStruct((n, d_h), F32)],
        compiler_params=pltpu.CompilerParams(
            dimension_semantics=("arbitrary",), vmem_limit_bytes=VMEM_LIMIT_BYTES),
        name="ffn_proj_block",
    )(x, *consts)


_STREAM_SLOT0 = (0, 1, 5)
_N_STREAM_SLOTS = 21
_WIDE = DILATIONS[-1]


def _attn_prompt_kernel(q_ref, k_ref, v_ref, o_ref, kt_ref, vt_ref, kprev, vprev, bias_scr, m_run, l_run, acc_run,
                        perm_scr):
    t = pl.program_id(2)

    @pl.when(t == 0)
    def _():
        kprev[...] = jnp.zeros_like(kprev)
        vprev[...] = jnp.zeros_like(vprev)

    @pl.when(t == pl.num_programs(2) - 1)
    def _():
        kt_ref[...] = k_ref[...].T.reshape(kt_ref.shape)
        vt_ref[...] = v_ref[...].T.reshape(vt_ref.shape)

    lane = lax.broadcasted_iota(jnp.int32, (N_SUB, LANES), 1)
    head0 = lane < HEAD_DIM
    key = lax.broadcasted_iota(jnp.int32, (2 * N_SUB, N_SUB), 0)
    qry = lax.broadcasted_iota(jnp.int32, (2 * N_SUB, N_SUB), 1)
    is_prev = key < N_SUB
    valid = jnp.logical_or(jnp.logical_and(is_prev, key >= qry),
                           jnp.logical_and(jnp.logical_not(is_prev), (key - N_SUB) <= qry))
    neg = jnp.full((2 * N_SUB, N_SUB), NEG_INF, F32)
    zeros = jnp.zeros((2 * N_SUB, N_SUB), F32)
    bias_scr[0] = jnp.where(valid, zeros, neg).astype(BF16)
    bias_scr[1] = jnp.where(jnp.logical_and(valid, jnp.logical_not(is_prev)), zeros, neg).astype(BF16)
    one_hot = ((key & (N_SUB - 1)) == qry).astype(F32).astype(BF16)
    ones_blk = jnp.ones((2 * N_SUB, LANES), BF16)

    out_row = lax.broadcasted_iota(jnp.int32, (N_SUB, N_SUB), 0)
    in_row = lax.broadcasted_iota(jnp.int32, (N_SUB, N_SUB), 1)
    perm = (in_row == (out_row & (SUBLANES - 1)) * _WIDE + (out_row >> 3)).astype(F32).astype(BF16)
    for ti, (ref, scale) in enumerate(((q_ref, ATTN_SCALE * LOG2_E), (k_ref, None), (v_ref, None))):
        for g in range(0, UNITS, 2):
            pair = []
            for gg in (g, g + 1):
                blk = ref[gg * N_SUB:(gg + 1) * N_SUB, :]
                pair.append((blk * scale if scale is not None else blk).astype(BF16))
            y = jnp.dot(perm, jnp.concatenate(pair, axis=1), preferred_element_type=F32)
            perm_scr[ti, g] = y[:, :LANES]
            perm_scr[ti, g + 1] = y[:, LANES:]

    for bi, d in enumerate(DILATIONS):
        blocks_per_stream = UNITS // d
        shift = int(math.log2(blocks_per_stream))

        def unit(u, carry, bi=bi, d=d, blocks_per_stream=blocks_per_stream, shift=shift):
            r = u >> shift
            n = u & (blocks_per_stream - 1)
            if d == 1:
                idx = pl.ds(pl.multiple_of(u * N_SUB, N_SUB), N_SUB)
            else:
                idx = pl.ds(r + d * N_SUB * n, N_SUB, stride=d)
            slot = _STREAM_SLOT0[bi] + r
            if d == _WIDE:
                sub = pl.ds(pl.multiple_of(r * SUBLANES, SUBLANES), SUBLANES)
                q = perm_scr[0, :, sub, :].reshape(N_SUB, LANES)
                kc = perm_scr[1, :, sub, :].reshape(N_SUB, LANES).astype(BF16)
                vc = perm_scr[2, :, sub, :].reshape(N_SUB, LANES).astype(BF16)
            else:
                q = q_ref[idx, :] * (ATTN_SCALE * LOG2_E)
                kc = k_ref[idx, :].astype(BF16)
                vc = v_ref[idx, :].astype(BF16)
            kp = kprev[slot]
            vp = vprev[slot]
            kprev[slot] = kc
            vprev[slot] = vc
            zero = jnp.zeros_like(q)
            q2 = jnp.concatenate([jnp.where(head0, q, zero), jnp.where(head0, zero, q)], axis=0).astype(BF16)
            first = jnp.logical_and(t == 0, n == 0).astype(jnp.int32)
            q_aug = jnp.concatenate([q2, one_hot], axis=1)
            k_aug = jnp.concatenate([jnp.concatenate([kp, kc], axis=0), bias_scr[first]], axis=1)
            v_aug = jnp.concatenate([jnp.concatenate([vp, vc], axis=0), ones_blk], axis=1)
            s = lax.dot_general(q_aug, k_aug, (((1,), (1,)), ((), ())), preferred_element_type=F32)
            m = jnp.max(s, axis=-1, keepdims=True)
            p = jnp.exp2(s - m)
            pv = jnp.dot(p.astype(BF16), v_aug, preferred_element_type=F32)
            o_un = jnp.where(head0, pv[:N_SUB, :LANES], pv[N_SUB:, :LANES])
            mb = jnp.where(head0, m[:N_SUB], m[N_SUB:])
            lb = jnp.where(head0, pv[:N_SUB, LANES:], pv[N_SUB:, LANES:])
            if bi == 0:
                m_run[idx, :] = mb
                l_run[idx, :] = lb
                acc_run[idx, :] = o_un
            else:
                m_old = m_run[idx, :]
                m_new = jnp.maximum(m_old, mb)
                a = jnp.exp2(m_old - m_new)
                b = jnp.exp2(mb - m_new)
                l_new = a * l_run[idx, :] + b * lb
                acc_new = a * acc_run[idx, :] + b * o_un
                if bi < len(DILATIONS) - 1:
                    m_run[idx, :] = m_new
                    l_run[idx, :] = l_new
                    acc_run[idx, :] = acc_new
                else:
                    o_ref[idx, :] = acc_new / l_new
            return carry

        lax.fori_loop(0, UNITS, unit, 0, unroll=ATTN_UNROLL)


def _attn_prompt(h, batch, seq):
    assert seq % ATTN_TILE == 0
    tiles = seq // ATTN_TILE
    pairs = D_A // LANES
    blk = (ATTN_TILE, LANES)
    heads_per_pair = LANES // HEAD_DIM
    state_spec = pl.BlockSpec((None, heads_per_pair, HEAD_DIM, ATTN_TILE), lambda b, hp, t: (b, hp, 0, 0))
    state_shape = jax.ShapeDtypeStruct((batch, N_HEADS, HEAD_DIM, ATTN_TILE), F32)
    return pl.pallas_call(
        _attn_prompt_kernel,
        grid=(batch, pairs, tiles),
        in_specs=[
            pl.BlockSpec(blk, lambda b, hp, t: (b * tiles + t, hp)),
            pl.BlockSpec(blk, lambda b, hp, t: (b * tiles + t, pairs + hp)),
            pl.BlockSpec(blk, lambda b, hp, t: (b * tiles + t, 2 * pairs + hp)),
        ],
        out_specs=[pl.BlockSpec(blk, lambda b, hp, t: (b * tiles + t, hp)), state_spec, state_spec],
        out_shape=[jax.ShapeDtypeStruct((batch * seq, D_A), F32), state_shape, state_shape],
        scratch_shapes=[
            pltpu.VMEM((_N_STREAM_SLOTS, N_SUB, LANES), BF16),
            pltpu.VMEM((_N_STREAM_SLOTS, N_SUB, LANES), BF16),
            pltpu.VMEM((2, 2 * N_SUB, N_SUB), BF16),
            pltpu.VMEM(blk, F32),
            pltpu.VMEM(blk, F32),
            pltpu.VMEM(blk, F32),
            pltpu.VMEM((3, UNITS, N_SUB, LANES), F32),
        ],
        compiler_params=pltpu.CompilerParams(
            dimension_semantics=("arbitrary", "arbitrary", "arbitrary"), vmem_limit_bytes=VMEM_LIMIT_BYTES),
        name="attn_prompt",
    )(h, h, h)


def _sample_key_multiplicity(w_buf, n_new):
    cnt = np.zeros((KV_ROWS, w_buf), np.float32)
    j = np.arange(w_buf)
    for t in range(n_new):
        dist = w_buf + t - j
        for d in DILATIONS:
            cnt[t] += ((dist % d == 0) & (dist // d >= 1) & (dist // d <= N_SUB)).astype(np.float32)
    return cnt


def _sample_kv_step(q_ref, kn_ref, vn_ref, cnt_ref, kt_ref, vt_ref, o_ref, kto_ref, vto_ref):
    w_buf = kt_ref.shape[-1]
    rows = cnt_ref.shape[0]
    n_new, width = q_ref.shape
    cnt = cnt_ref[...]
    valid = cnt > 0.0
    pad = jnp.zeros((rows - n_new, width), F32)
    q16 = jnp.concatenate([q_ref[...] * ATTN_SCALE, pad], axis=0)
    kn16 = jnp.concatenate([kn_ref[...], pad], axis=0)
    vn16 = jnp.concatenate([vn_ref[...], pad], axis=0)
    tpad = jnp.zeros((LANES - rows, width), F32)
    kn_t = jnp.concatenate([kn16, tpad], axis=0).T
    vn_t = jnp.concatenate([vn16, tpad], axis=0).T
    rr = lax.broadcasted_iota(jnp.int32, (rows, rows), 0)
    cc = lax.broadcasted_iota(jnp.int32, (rows, rows), 1)
    cnt_new = jnp.where(cc == rr, float(len(DILATIONS)), jnp.where(cc < rr, 1.0, 0.0))
    valid_new = cnt_new > 0.0
    outs = []
    for h in range(HEADS_PER_KV_STEP):
        cols = slice(h * HEAD_DIM, (h + 1) * HEAD_DIM)
        kth = kt_ref[h]
        vth = vt_ref[h]
        qh = q16[:, cols].astype(BF16)
        knh = kn16[:, cols].astype(BF16)
        vnh = vn16[:, cols].astype(BF16)
        s = jnp.dot(qh, kth.astype(BF16), preferred_element_type=F32)
        s_new = lax.dot_general(qh, knh, (((1,), (1,)), ((), ())), preferred_element_type=F32)
        s = jnp.where(valid, s, NEG_INF)
        s_new = jnp.where(valid_new, s_new, NEG_INF)
        m = jnp.maximum(jnp.max(s, axis=-1, keepdims=True), jnp.max(s_new, axis=-1, keepdims=True))
        p = jnp.exp(s - m) * cnt
        p_new = jnp.exp(s_new - m) * cnt_new
        l = jnp.sum(p, axis=-1, keepdims=True) + jnp.sum(p_new, axis=-1, keepdims=True)
        pv = lax.dot_general(p.astype(BF16), vth.astype(BF16), (((1,), (1,)), ((), ())),
                             preferred_element_type=F32)
        pv = pv + jnp.dot(p_new.astype(BF16), vnh, preferred_element_type=F32)
        outs.append((pv / l)[:n_new])
        for src, new_t, dst in ((kth, kn_t, kto_ref), (vth, vn_t, vto_ref)):
            dst[h, :, :w_buf - n_new] = src[:, n_new:]
            dst[h, :, w_buf - n_new:] = new_t[cols, :n_new]
    o_ref[...] = jnp.concatenate(outs, axis=-1)


def _mix_half_step(x_ref, oa_ref, u_ref, v_ref, sg_ref, sbias_ref, vg_ref, vb_ref, ga_ref, gb_ref,
                   wout_ref, g2_ref, b2_ref, alpha, sample):
    vn = _layer_norm(v_ref[...], vg_ref[...], vb_ref[...])
    n_chunks = vn.shape[0] // CHUNK
    if sample:
        blocks = []
        for ti in range(n_chunks):
            acc = sbias_ref[ti:ti + 1, :]
            for j in range(ti + 1):
                acc = acc + sg_ref[ti * n_chunks + j:ti * n_chunks + j + 1, :] * vn[j * CHUNK:(j + 1) * CHUNK, :]
            blocks.append(acc)
        mix = jnp.concatenate(blocks, axis=0)
    else:
        ri = lax.broadcasted_iota(jnp.int32, (CHUNK, CHUNK), 0)
        ci = lax.broadcasted_iota(jnp.int32, (CHUNK, CHUNK), 1)
        causal = ci <= ri
        vnb = vn.astype(BF16)
        per_group = []
        for g in range(N_GROUPS_B):
            w = jnp.where(causal, sg_ref[g], 0.0).astype(BF16)
            lanes = slice(g * GROUP_B, (g + 1) * GROUP_B)
            rhs = jnp.concatenate([vnb[c * CHUNK:(c + 1) * CHUNK, lanes] for c in range(n_chunks)], axis=1)
            per_group.append(jnp.dot(w, rhs, preferred_element_type=F32))
        mix = jnp.concatenate(
            [jnp.concatenate([per_group[g][:, c * CHUNK:(c + 1) * CHUNK] for g in range(N_GROUPS_B)], axis=1)
             + sbias_ref[...] for c in range(n_chunks)], axis=0)
    o_b = u_ref[...] * mix
    cat = jnp.concatenate([_rms_norm(oa_ref[...], ga_ref[...]), _rms_norm(o_b, gb_ref[...])], axis=-1)
    mixed = jnp.dot(cat.astype(BF16), wout_ref[...], preferred_element_type=F32)
    return _layer_norm(alpha * x_ref[...] + mixed, g2_ref[...], b2_ref[...]), vn


def _mix_ffn_sample_kernel(*refs, alpha):
    mix_refs, (win2_ref, wout2_ref, g3_ref, b3_ref, y_ref, vn_ref) = refs[:13], refs[13:]
    x2, vn = _mix_half_step(*mix_refs, alpha, sample=True)
    vn_ref[...] = vn
    y_ref[...] = _ffn_half_step(x2, win2_ref, wout2_ref, g3_ref, b3_ref, alpha)


def _mix_ffn_sample(x1, o_a, h, mix_consts, ffn_consts, alpha):
    n = x1.shape[0]
    u_blk = 3 * D_A // D_B
    row = lambda width, j: pl.BlockSpec((n, width), lambda i, j=j: (0, j))
    consts = list(mix_consts) + list(ffn_consts)
    return pl.pallas_call(
        functools.partial(_mix_ffn_sample_kernel, alpha=alpha),
        grid=(1,),
        in_specs=[row(D_MODEL, 0), row(D_A, 0), row(D_B, u_blk), row(D_B, u_blk + 1)]
        + [_const_spec(c.shape) for c in consts],
        out_specs=[row(D_MODEL, 0), row(D_B, 0)],
        out_shape=[jax.ShapeDtypeStruct((n, D_MODEL), F32), jax.ShapeDtypeStruct((n, D_B), F32)],
        compiler_params=pltpu.CompilerParams(
            dimension_semantics=("arbitrary",), vmem_limit_bytes=VMEM_LIMIT_BYTES),
        name="mix_ffn_sample",
    )(x1, o_a, h, h, *consts)


def _prompt_tail_kernel(*refs, alpha):
    mix_refs = refs[:13]
    win2_ref, wout2_ref, g3_ref, b3_ref = refs[13:17]
    kv_in = refs[17:23]
    y_ref = refs[23]
    kv_out = refs[24:27]
    x2_scr = refs[27]

    @pl.when(pl.program_id(0) == 0)
    def _():
        x2_scr[...] = jnp.zeros_like(x2_scr)

    x2_prev = x2_scr[...]
    x2_next, _ = _mix_half_step(*mix_refs, alpha, sample=False)
    y_ref[...] = _ffn_half_step(x2_prev, win2_ref, wout2_ref, g3_ref, b3_ref, alpha)
    x2_scr[...] = x2_next
    _sample_kv_step(*kv_in, *kv_out)


def _prompt_tail(x1, o_a, h, mix_consts, ffn_consts, h_b, kt, vt, alpha):
    n = x1.shape[0]
    assert n % TAIL_TILE == 0
    n_tiles = n // TAIL_TILE
    dec_batch, n_new, _ = h_b.shape
    w_buf = kt.shape[-1]
    groups = N_HEADS // HEADS_PER_KV_STEP
    width = HEADS_PER_KV_STEP * HEAD_DIM
    n_side = dec_batch * groups
    steps = max(n_tiles + 1, n_side)
    cnt = jnp.asarray(_sample_key_multiplicity(w_buf, n_new))
    u_blk = 3 * D_A // D_B

    tile_in = lambda i: jnp.minimum(i, n_tiles - 1)
    row = lambda w, j: pl.BlockSpec((TAIL_TILE, w), lambda i, j=j: (tile_in(i), j))
    side = lambda i: jnp.minimum(i, n_side - 1)
    new_spec = lambda off: pl.BlockSpec(
        (None, n_new, width), lambda i, off=off: (side(i) // groups, 0, off + side(i) % groups))
    kv_spec = pl.BlockSpec((None, HEADS_PER_KV_STEP, HEAD_DIM, w_buf),
                           lambda i: (side(i) // groups, side(i) % groups, 0, 0))
    consts = list(mix_consts) + list(ffn_consts)
    return pl.pallas_call(
        functools.partial(_prompt_tail_kernel, alpha=alpha),
        grid=(steps,),
        in_specs=[row(D_MODEL, 0), row(D_A, 0), row(D_B, u_blk), row(D_B, u_blk + 1)]
        + [_const_spec(c.shape) for c in consts]
        + [new_spec(0), new_spec(groups), new_spec(2 * groups), _const_spec(cnt.shape), kv_spec, kv_spec],
        out_specs=[pl.BlockSpec((TAIL_TILE, D_MODEL), lambda i: (jnp.clip(i - 1, 0, n_tiles - 1), 0)),
                   new_spec(0), kv_spec, kv_spec],
        out_shape=[
            jax.ShapeDtypeStruct((n, D_MODEL), F32),
            jax.ShapeDtypeStruct((dec_batch, n_new, D_A), F32),
            jax.ShapeDtypeStruct(kt.shape, F32),
            jax.ShapeDtypeStruct(vt.shape, F32),
        ],
        scratch_shapes=[pltpu.VMEM((TAIL_TILE, D_MODEL), F32)],
        compiler_params=pltpu.CompilerParams(
            dimension_semantics=("arbitrary",), vmem_limit_bytes=VMEM_LIMIT_BYTES),
        name="prompt_tail",
    )(x1, o_a, h, h, *consts, h_b, h_b, h_b, cnt, kt, vt)


def kernel(x_prompt, x_sample, cache_k, cache_v, ffn1_w_in, ffn1_w_out, ln1_g, ln1_b, w_in, sgu_w, sgu_b,
           sgu_v_g, sgu_v_b, out_a_g, out_b_g, w_out, ln2_g, ln2_b, ffn2_w_in, ffn2_w_out, ln3_g, ln3_b):
    depth = ffn1_w_in.shape[0]
    batch, seq, _ = x_prompt.shape
    dec_batch, n_new, _ = x_sample.shape
    w_buf = cache_k.shape[2]
    assert dec_batch == CHUNK and seq % ATTN_TILE == 0 and w_buf == W_MAX
    alpha = (2.0 * depth) ** 0.25
    assert ATTN_TILE == min(W_MAX, seq)

    xp = x_prompt.reshape(batch * seq, D_MODEL)
    xs = jnp.transpose(x_sample, (1, 0, 2)).reshape(n_new * dec_batch, D_MODEL)
    kp_l, vp_l, ks_l, vs_l, us_l = [], [], [], [], []
    tril = jnp.tril(jnp.ones((CHUNK, CHUNK), bool))
    for layer in range(depth):
        w1i, w1o = ffn1_w_in[layer].astype(BF16), ffn1_w_out[layer].astype(BF16)
        w2i, w2o = ffn2_w_in[layer].astype(BF16), ffn2_w_out[layer].astype(BF16)
        wp, wo = w_in[layer].astype(BF16), w_out[layer].astype(BF16)
        row = lambda a: a[layer][None, :]
        g1, b1, g2, b2, g3, b3 = row(ln1_g), row(ln1_b), row(ln2_g), row(ln2_b), row(ln3_g), row(ln3_b)
        vg, vb, ga, gb = row(sgu_v_g), row(sgu_v_b), row(out_a_g), row(out_b_g)
        sgw = sgu_w[layer]
        sg_bias = jnp.repeat(sgu_b[layer].T, GROUP_B, axis=1)
        corner = jnp.where(tril[:n_new, :n_new], sgw[:, :n_new, :n_new], 0.0)
        coef = jnp.repeat(jnp.transpose(corner, (1, 2, 0)).reshape(n_new * n_new, N_GROUPS_B), GROUP_B, axis=1)

        ffn2_consts = (w2i, w2o, g3, b3)
        mix_tail = (vg, vb, ga, gb, wo, g2, b2)

        x1, h = _ffn_proj_block(xp, w1i, w1o, g1, b1, wp, alpha)
        x1s, hs = _ffn_proj_block(xs, w1i, w1o, g1, b1, wp, alpha)
        hs_b = jnp.transpose(hs.reshape(n_new, dec_batch, -1), (1, 0, 2))
        kt = jnp.transpose(cache_k[layer], (0, 2, 3, 1))
        vt = jnp.transpose(cache_v[layer], (0, 2, 3, 1))

        o_a, kt_p, vt_p = _attn_prompt(h, batch, seq)
        xp, o_s, kt_new, vt_new = _prompt_tail(x1, o_a, h, (sgw, sg_bias) + mix_tail, ffn2_consts, hs_b, kt, vt, alpha)
        kp_l.append(jnp.transpose(kt_p, (0, 3, 1, 2)))
        vp_l.append(jnp.transpose(vt_p, (0, 3, 1, 2)))

        o_s = jnp.transpose(o_s, (1, 0, 2)).reshape(n_new * dec_batch, D_A)
        xs, vn_s = _mix_ffn_sample(x1s, o_s, hs, (coef, sg_bias[:n_new]) + mix_tail, ffn2_consts, alpha)
        ks_l.append(jnp.transpose(kt_new, (0, 3, 1, 2)))
        vs_l.append(jnp.transpose(vt_new, (0, 3, 1, 2)))
        us_l.append(jnp.transpose(vn_s.reshape(n_new, dec_batch, D_B), (1, 0, 2)))

    y_p = xp.reshape(batch, seq, D_MODEL)
    y_s = jnp.transpose(xs.reshape(n_new, dec_batch, D_MODEL), (1, 0, 2))
    return (y_p, y_s, jnp.stack(kp_l), jnp.stack(vp_l), jnp.stack(ks_l), jnp.stack(vs_l), jnp.stack(us_l))
```

```python
import functools
import math

import numpy as np
import jax
import jax.numpy as jnp
from jax import lax
from jax.experimental import pallas as pl
from jax.experimental.pallas import tpu as pltpu

F32 = jnp.float32
BF16 = jnp.bfloat16

D_MODEL = 1024
HEAD_DIM = 64
N_HEADS = 8
D_A = N_HEADS * HEAD_DIM
D_B = D_MODEL - D_A
D_FF = 2816
CHUNK = 128
GROUP_B = 128
N_GROUPS_B = D_B // GROUP_B
N_SUB = 128
DILATIONS = (1, 4, 16)
W_MAX = 2048
ATTN_SCALE = HEAD_DIM ** -0.5
LOG2_E = math.log2(math.e)
LN_EPS = 1e-5
NEG_INF = -1e30
FFN_HALF = 0.5

LANES = 128
SUBLANES = 8
VMEM_LIMIT_BYTES = 56 * 1024 * 1024

TOKEN_TILE = 512
TAIL_TILE = 256
FF_CHUNKS = ((0, 1536), (1536, 1280))
ATTN_TILE = 2048
UNITS = ATTN_TILE // N_SUB
ATTN_UNROLL = 16
HEADS_PER_KV_STEP = 4
KV_ROWS = 8


def _const_spec(shape):
    nd = len(shape)
    return pl.BlockSpec(shape, lambda *_: (0,) * nd, pipeline_mode=pl.Buffered(1))


def _layer_norm(x, g, b):
    mu = jnp.mean(x, axis=-1, keepdims=True)
    xc = x - mu
    var = jnp.mean(xc * xc, axis=-1, keepdims=True)
    return xc * lax.rsqrt(var + LN_EPS) * g + b


def _rms_norm(x, g):
    return x * lax.rsqrt(jnp.mean(x * x, axis=-1, keepdims=True) + LN_EPS) * g


def _ffn_half_step(x, win_ref, wout_ref, g_ref, b_ref, alpha):
    xb = x.astype(BF16)
    acc = None
    for c0, cw in FF_CHUNKS:
        gate = jnp.dot(xb, win_ref[:, c0:c0 + cw], preferred_element_type=F32)
        up = jnp.dot(xb, win_ref[:, D_FF + c0:D_FF + c0 + cw], preferred_element_type=F32)
        act = (gate * (1.0 / (1.0 + jnp.exp(-gate))) * up).astype(BF16)
        part = jnp.dot(act, wout_ref[c0:c0 + cw, :], preferred_element_type=F32)
        acc = part if acc is None else acc + part
    return _layer_norm(alpha * x + FFN_HALF * acc, g_ref[...], b_ref[...])


def _ffn_proj_kernel(x_ref, win_ref, wout_ref, g_ref, b_ref, wp_ref, y_ref, h_ref, *, alpha):
    y = _ffn_half_step(x_ref[...], win_ref, wout_ref, g_ref, b_ref, alpha)
    y_ref[...] = y
    h_ref[...] = jnp.dot(y.astype(BF16), wp_ref[...], preferred_element_type=F32)


def _ffn_proj_block(x, w_in, w_out, g, b, w_proj, alpha):
    n = x.shape[0]
    assert n % TOKEN_TILE == 0
    d_h = w_proj.shape[1]
    row_spec = pl.BlockSpec((TOKEN_TILE, D_MODEL), lambda i: (i, 0))
    consts = [w_in, w_out, g, b, w_proj]
    return pl.pallas_call(
        functools.partial(_ffn_proj_kernel, alpha=alpha),
        grid=(n // TOKEN_TILE,),
        in_specs=[row_spec] + [_const_spec(c.shape) for c in consts],
        out_specs=[row_spec, pl.BlockSpec((TOKEN_TILE, d_h), lambda i: (i, 0))],
        out_shape=[jax.ShapeDtypeStruct((n, D_MODEL), F32), jax.ShapeDtypeStruct((n, d_h), F32)],
        compiler_params=pltpu.CompilerParams(
            dimension_semantics=("arbitrary",), vmem_limit_bytes=VMEM_LIMIT_BYTES),
        name="ffn_proj_block",
    )(x, *consts)


_STREAM_SLOT0 = (0, 1, 5)
_N_STREAM_SLOTS = 21
_WIDE = DILATIONS[-1]


def _attn_prompt_kernel(q_ref, k_ref, v_ref, o_ref, kt_ref, vt_ref, kprev, vprev, bias_scr, m_run, l_run, acc_run,
                        perm_scr):
    t = pl.program_id(2)

    @pl.when(t == 0)
    def _():
        kprev[...] = jnp.zeros_like(kprev)
        vprev[...] = jnp.zeros_like(vprev)

    @pl.when(t == pl.num_programs(2) - 1)
    def _():
        kt_ref[...] = k_ref[...].T.reshape(kt_ref.shape)
        vt_ref[...] = v_ref[...].T.reshape(vt_ref.shape)

    lane = lax.broadcasted_iota(jnp.int32, (N_SUB, LANES), 1)
    head0 = lane < HEAD_DIM
    key = lax.broadcasted_iota(jnp.int32, (2 * N_SUB, N_SUB), 0)
    qry = lax.broadcasted_iota(jnp.int32, (2 * N_SUB, N_SUB), 1)
    is_prev = key < N_SUB
    valid = jnp.logical_or(jnp.logical_and(is_prev, key >= qry),
                           jnp.logical_and(jnp.logical_not(is_prev), (key - N_SUB) <= qry))
    neg = jnp.full((2 * N_SUB, N_SUB), NEG_INF, F32)
    zeros = jnp.zeros((2 * N_SUB, N_SUB), F32)
    bias_scr[0] = jnp.where(valid, zeros, neg).astype(BF16)
    bias_scr[1] = jnp.where(jnp.logical_and(valid, jnp.logical_not(is_prev)), zeros, neg).astype(BF16)
    one_hot = ((key & (N_SUB - 1)) == qry).astype(F32).astype(BF16)
    ones_blk = jnp.ones((2 * N_SUB, LANES), BF16)

    out_row = lax.broadcasted_iota(jnp.int32, (N_SUB, N_SUB), 0)
    in_row = lax.broadcasted_iota(jnp.int32, (N_SUB, N_SUB), 1)
    perm = (in_row == (out_row & (SUBLANES - 1)) * _WIDE + (out_row >> 3)).astype(F32).astype(BF16)
    for ti, (ref, scale) in enumerate(((q_ref, ATTN_SCALE * LOG2_E), (k_ref, None), (v_ref, None))):
        for g in range(0, UNITS, 2):
            pair = []
            for gg in (g, g + 1):
                blk = ref[gg * N_SUB:(gg + 1) * N_SUB, :]
                pair.append((blk * scale if scale is not None else blk).astype(BF16))
            y = jnp.dot(perm, jnp.concatenate(pair, axis=1), preferred_element_type=F32)
            perm_scr[ti, g] = y[:, :LANES]
            perm_scr[ti, g + 1] = y[:, LANES:]

    for oi, bi in enumerate(reversed(range(len(DILATIONS)))):
        d = DILATIONS[bi]
        blocks_per_stream = UNITS // d
        shift = int(math.log2(blocks_per_stream))

        def unit(u, carry, oi=oi, bi=bi, d=d, blocks_per_stream=blocks_per_stream, shift=shift):
            r = u >> shift
            n = u & (blocks_per_stream - 1)
            if d == 1:
                idx = pl.ds(pl.multiple_of(u * N_SUB, N_SUB), N_SUB)
            else:
                idx = pl.ds(r + d * N_SUB * n, N_SUB, stride=d)
            slot = _STREAM_SLOT0[bi] + r
            if d == _WIDE:
                sub = pl.ds(pl.multiple_of(r * SUBLANES, SUBLANES), SUBLANES)
                q = perm_scr[0, :, sub, :].reshape(N_SUB, LANES)
                kc = perm_scr[1, :, sub, :].reshape(N_SUB, LANES).astype(BF16)
                vc = perm_scr[2, :, sub, :].reshape(N_SUB, LANES).astype(BF16)
            else:
                q = q_ref[idx, :] * (ATTN_SCALE * LOG2_E)
                kc = k_ref[idx, :].astype(BF16)
                vc = v_ref[idx, :].astype(BF16)
            kp = kprev[slot]
            vp = vprev[slot]
            kprev[slot] = kc
            vprev[slot] = vc
            zero = jnp.zeros_like(q)
            q2 = jnp.concatenate([jnp.where(head0, q, zero), jnp.where(head0, zero, q)], axis=0).astype(BF16)
            first = jnp.logical_and(t == 0, n == 0).astype(jnp.int32)
            q_aug = jnp.concatenate([q2, one_hot], axis=1)
            k_aug = jnp.concatenate([jnp.concatenate([kp, kc], axis=0), bias_scr[first]], axis=1)
            v_aug = jnp.concatenate([jnp.concatenate([vp, vc], axis=0), ones_blk], axis=1)
            s = lax.dot_general(q_aug, k_aug, (((1,), (1,)), ((), ())), preferred_element_type=F32)
            m = jnp.max(s, axis=-1, keepdims=True)
            p = jnp.exp2(s - m)
            pv = jnp.dot(p.astype(BF16), v_aug, preferred_element_type=F32)
            o_un = jnp.where(head0, pv[:N_SUB, :LANES], pv[N_SUB:, :LANES])
            mb = jnp.where(head0, m[:N_SUB], m[N_SUB:])
            lb = jnp.where(head0, pv[:N_SUB, LANES:], pv[N_SUB:, LANES:])
            if oi == 0:
                m_run[idx, :] = mb
                l_run[idx, :] = lb
                acc_run[idx, :] = o_un
            else:
                m_old = m_run[idx, :]
                m_new = jnp.maximum(m_old, mb)
                a = jnp.exp2(m_old - m_new)
                b = jnp.exp2(mb - m_new)
                l_new = a * l_run[idx, :] + b * lb
                acc_new = a * acc_run[idx, :] + b * o_un
                if oi < len(DILATIONS) - 1:
                    m_run[idx, :] = m_new
                    l_run[idx, :] = l_new
                    acc_run[idx, :] = acc_new
                else:
                    o_ref[idx, :] = acc_new / l_new
            return carry

        lax.fori_loop(0, UNITS, unit, 0, unroll=ATTN_UNROLL)


def _attn_prompt(h, batch, seq):
    assert seq % ATTN_TILE == 0
    tiles = seq // ATTN_TILE
    pairs = D_A // LANES
    blk = (ATTN_TILE, LANES)
    heads_per_pair = LANES // HEAD_DIM
    state_spec = pl.BlockSpec((None, heads_per_pair, HEAD_DIM, ATTN_TILE), lambda b, hp, t: (b, hp, 0, 0))
    state_shape = jax.ShapeDtypeStruct((batch, N_HEADS, HEAD_DIM, ATTN_TILE), F32)
    return pl.pallas_call(
        _attn_prompt_kernel,
        grid=(batch, pairs, tiles),
        in_specs=[
            pl.BlockSpec(blk, lambda b, hp, t: (b * tiles + t, hp)),
            pl.BlockSpec(blk, lambda b, hp, t: (b * tiles + t, pairs + hp)),
            pl.BlockSpec(blk, lambda b, hp, t: (b * tiles + t, 2 * pairs + hp)),
        ],
        out_specs=[pl.BlockSpec(blk, lambda b, hp, t: (b * tiles + t, hp)), state_spec, state_spec],
        out_shape=[jax.ShapeDtypeStruct((batch * seq, D_A), F32), state_shape, state_shape],
        scratch_shapes=[
            pltpu.VMEM((_N_STREAM_SLOTS, N_SUB, LANES), BF16),
            pltpu.VMEM((_N_STREAM_SLOTS, N_SUB, LANES), BF16),
            pltpu.VMEM((2, 2 * N_SUB, N_SUB), BF16),
            pltpu.VMEM(blk, F32),
            pltpu.VMEM(blk, F32),
            pltpu.VMEM(blk, F32),
            pltpu.VMEM((3, UNITS, N_SUB, LANES), F32),
        ],
        compiler_params=pltpu.CompilerParams(
            dimension_semantics=("arbitrary", "arbitrary", "arbitrary"), vmem_limit_bytes=VMEM_LIMIT_BYTES),
        name="attn_prompt",
    )(h, h, h)


def _sample_key_multiplicity(w_buf, n_new):
    cnt = np.zeros((KV_ROWS, w_buf), np.float32)
    j = np.arange(w_buf)
    for t in range(n_new):
        dist = w_buf + t - j
        for d in DILATIONS:
            cnt[t] += ((dist % d == 0) & (dist // d >= 1) & (dist // d <= N_SUB)).astype(np.float32)
    return cnt


def _sample_kv_step(q_ref, kn_ref, vn_ref, cnt_ref, kt_ref, vt_ref, o_ref, kto_ref, vto_ref):
    w_buf = kt_ref.shape[-1]
    rows = cnt_ref.shape[0]
    n_new, width = q_ref.shape
    cnt = cnt_ref[...]
    valid = cnt > 0.0
    pad = jnp.zeros((rows - n_new, width), F32)
    q16 = jnp.concatenate([q_ref[...] * ATTN_SCALE, pad], axis=0)
    kn16 = jnp.concatenate([kn_ref[...], pad], axis=0)
    vn16 = jnp.concatenate([vn_ref[...], pad], axis=0)
    tpad = jnp.zeros((LANES - rows, width), F32)
    kn_t = jnp.concatenate([kn16, tpad], axis=0).T
    vn_t = jnp.concatenate([vn16, tpad], axis=0).T
    rr = lax.broadcasted_iota(jnp.int32, (rows, rows), 0)
    cc = lax.broadcasted_iota(jnp.int32, (rows, rows), 1)
    cnt_new = jnp.where(cc == rr, float(len(DILATIONS)), jnp.where(cc < rr, 1.0, 0.0))
    valid_new = cnt_new > 0.0
    outs = []
    for h in range(HEADS_PER_KV_STEP):
        cols = slice(h * HEAD_DIM, (h + 1) * HEAD_DIM)
        kth = kt_ref[h]
        vth = vt_ref[h]
        qh = q16[:, cols].astype(BF16)
        knh = kn16[:, cols].astype(BF16)
        vnh = vn16[:, cols].astype(BF16)
        s = jnp.dot(qh, kth.astype(BF16), preferred_element_type=F32)
        s_new = lax.dot_general(qh, knh, (((1,), (1,)), ((), ())), preferred_element_type=F32)
        s = jnp.where(valid, s, NEG_INF)
        s_new = jnp.where(valid_new, s_new, NEG_INF)
        m = jnp.maximum(jnp.max(s, axis=-1, keepdims=True), jnp.max(s_new, axis=-1, keepdims=True))
        p = jnp.exp(s - m) * cnt
        p_new = jnp.exp(s_new - m) * cnt_new
        l = jnp.sum(p, axis=-1, keepdims=True) + jnp.sum(p_new, axis=-1, keepdims=True)
        pv = lax.dot_general(p.astype(BF16), vth.astype(BF16), (((1,), (1,)), ((), ())),
                             preferred_element_type=F32)
        pv = pv + jnp.dot(p_new.astype(BF16), vnh, preferred_element_type=F32)
        outs.append((pv / l)[:n_new])
        for src, new_t, dst in ((kth, kn_t, kto_ref), (vth, vn_t, vto_ref)):
            dst[h, :, :w_buf - n_new] = src[:, n_new:]
            dst[h, :, w_buf - n_new:] = new_t[cols, :n_new]
    o_ref[...] = jnp.concatenate(outs, axis=-1)


def _mix_half_step(x_ref, oa_ref, u_ref, v_ref, sg_ref, sbias_ref, vg_ref, vb_ref, ga_ref, gb_ref,
                   wout_ref, g2_ref, b2_ref, alpha, sample):
    vn = _layer_norm(v_ref[...], vg_ref[...], vb_ref[...])
    n_chunks = vn.shape[0] // CHUNK
    if sample:
        blocks = []
        for ti in range(n_chunks):
            acc = sbias_ref[ti:ti + 1, :]
            for j in range(ti + 1):
                acc = acc + sg_ref[ti * n_chunks + j:ti * n_chunks + j + 1, :] * vn[j * CHUNK:(j + 1) * CHUNK, :]
            blocks.append(acc)
        mix = jnp.concatenate(blocks, axis=0)
    else:
        ri = lax.broadcasted_iota(jnp.int32, (CHUNK, CHUNK), 0)
        ci = lax.broadcasted_iota(jnp.int32, (CHUNK, CHUNK), 1)
        causal = ci <= ri
        vnb = vn.astype(BF16)
        per_group = []
        for g in range(N_GROUPS_B):
            w = jnp.where(causal, sg_ref[g], 0.0).astype(BF16)
            lanes = slice(g * GROUP_B, (g + 1) * GROUP_B)
            rhs = jnp.concatenate([vnb[c * CHUNK:(c + 1) * CHUNK, lanes] for c in range(n_chunks)], axis=1)
            per_group.append(jnp.dot(w, rhs, preferred_element_type=F32))
        mix = jnp.concatenate(
            [jnp.concatenate([per_group[g][:, c * CHUNK:(c + 1) * CHUNK] for g in range(N_GROUPS_B)], axis=1)
             + sbias_ref[...] for c in range(n_chunks)], axis=0)
    o_b = u_ref[...] * mix
    cat = jnp.concatenate([_rms_norm(oa_ref[...], ga_ref[...]), _rms_norm(o_b, gb_ref[...])], axis=-1)
    mixed = jnp.dot(cat.astype(BF16), wout_ref[...], preferred_element_type=F32)
    return _layer_norm(alpha * x_ref[...] + mixed, g2_ref[...], b2_ref[...]), vn


def _mix_ffn_sample_kernel(*refs, alpha):
    mix_refs, (win2_ref, wout2_ref, g3_ref, b3_ref, y_ref, vn_ref) = refs[:13], refs[13:]
    x2, vn = _mix_half_step(*mix_refs, alpha, sample=True)
    vn_ref[...] = vn
    y_ref[...] = _ffn_half_step(x2, win2_ref, wout2_ref, g3_ref, b3_ref, alpha)


def _mix_ffn_sample(x1, o_a, h, mix_consts, ffn_consts, alpha):
    n = x1.shape[0]
    u_blk = 3 * D_A // D_B
    row = lambda width, j: pl.BlockSpec((n, width), lambda i, j=j: (0, j))
    consts = list(mix_consts) + list(ffn_consts)
    return pl.pallas_call(
        functools.partial(_mix_ffn_sample_kernel, alpha=alpha),
        grid=(1,),
        in_specs=[row(D_MODEL, 0), row(D_A, 0), row(D_B, u_blk), row(D_B, u_blk + 1)]
        + [_const_spec(c.shape) for c in consts],
        out_specs=[row(D_MODEL, 0), row(D_B, 0)],
        out_shape=[jax.ShapeDtypeStruct((n, D_MODEL), F32), jax.ShapeDtypeStruct((n, D_B), F32)],
        compiler_params=pltpu.CompilerParams(
            dimension_semantics=("arbitrary",), vmem_limit_bytes=VMEM_LIMIT_BYTES),
        name="mix_ffn_sample",
    )(x1, o_a, h, h, *consts)


def _prompt_tail_kernel(*refs, alpha):
    mix_refs = refs[:13]
    win2_ref, wout2_ref, g3_ref, b3_ref = refs[13:17]
    kv_in = refs[17:23]
    y_ref = refs[23]
    kv_out = refs[24:27]
    x2_scr = refs[27]

    @pl.when(pl.program_id(0) == 0)
    def _():
        x2_scr[...] = jnp.zeros_like(x2_scr)

    x2_prev = x2_scr[...]
    x2_next, _ = _mix_half_step(*mix_refs, alpha, sample=False)
    y_ref[...] = _ffn_half_step(x2_prev, win2_ref, wout2_ref, g3_ref, b3_ref, alpha)
    x2_scr[...] = x2_next
    _sample_kv_step(*kv_in, *kv_out)


def _prompt_tail(x1, o_a, h, mix_consts, ffn_consts, h_b, kt, vt, alpha):
    n = x1.shape[0]
    assert n % TAIL_TILE == 0
    n_tiles = n // TAIL_TILE
    dec_batch, n_new, _ = h_b.shape
    w_buf = kt.shape[-1]
    groups = N_HEADS // HEADS_PER_KV_STEP
    width = HEADS_PER_KV_STEP * HEAD_DIM
    n_side = dec_batch * groups
    steps = max(n_tiles + 1, n_side)
    cnt = jnp.asarray(_sample_key_multiplicity(w_buf, n_new))
    u_blk = 3 * D_A // D_B

    tile_in = lambda i: jnp.minimum(i, n_tiles - 1)
    row = lambda w, j: pl.BlockSpec((TAIL_TILE, w), lambda i, j=j: (tile_in(i), j))
    side = lambda i: jnp.minimum(i, n_side - 1)
    new_spec = lambda off: pl.BlockSpec(
        (None, n_new, width), lambda i, off=off: (side(i) // groups, 0, off + side(i) % groups))
    kv_spec = pl.BlockSpec((None, HEADS_PER_KV_STEP, HEAD_DIM, w_buf),
                           lambda i: (side(i) // groups, side(i) % groups, 0, 0))
    consts = list(mix_consts) + list(ffn_consts)
    return pl.pallas_call(
        functools.partial(_prompt_tail_kernel, alpha=alpha),
        grid=(steps,),
        in_specs=[row(D_MODEL, 0), row(D_A, 0), row(D_B, u_blk), row(D_B, u_blk + 1)]
        + [_const_spec(c.shape) for c in consts]
        + [new_spec(0), new_spec(groups), new_spec(2 * groups), _const_spec(cnt.shape), kv_spec, kv_spec],
        out_specs=[pl.BlockSpec((TAIL_TILE, D_MODEL), lambda i: (jnp.clip(i - 1, 0, n_tiles - 1), 0)),
                   new_spec(0), kv_spec, kv_spec],
        out_shape=[
            jax.ShapeDtypeStruct((n, D_MODEL), F32),
            jax.ShapeDtypeStruct((dec_batch, n_new, D_A), F32),
            jax.ShapeDtypeStruct(kt.shape, F32),
            jax.ShapeDtypeStruct(vt.shape, F32),
        ],
        scratch_shapes=[pltpu.VMEM((TAIL_TILE, D_MODEL), F32)],
        compiler_params=pltpu.CompilerParams(
            dimension_semantics=("arbitrary",), vmem_limit_bytes=VMEM_LIMIT_BYTES),
        name="prompt_tail",
    )(x1, o_a, h, h, *consts, h_b, h_b, h_b, cnt, kt, vt)


def kernel(x_prompt, x_sample, cache_k, cache_v, ffn1_w_in, ffn1_w_out, ln1_g, ln1_b, w_in, sgu_w, sgu_b,
           sgu_v_g, sgu_v_b, out_a_g, out_b_g, w_out, ln2_g, ln2_b, ffn2_w_in, ffn2_w_out, ln3_g, ln3_b):
    depth = ffn1_w_in.shape[0]
    batch, seq, _ = x_prompt.shape
    dec_batch, n_new, _ = x_sample.shape
    w_buf = cache_k.shape[2]
    assert dec_batch == CHUNK and seq % ATTN_TILE == 0 and w_buf == W_MAX
    alpha = (2.0 * depth) ** 0.25
    assert ATTN_TILE == min(W_MAX, seq)

    xp = x_prompt.reshape(batch * seq, D_MODEL)
    xs = jnp.transpose(x_sample, (1, 0, 2)).reshape(n_new * dec_batch, D_MODEL)
    kp_l, vp_l, ks_l, vs_l, us_l = [], [], [], [], []
    tril = jnp.tril(jnp.ones((CHUNK, CHUNK), bool))
    for layer in range(depth):
        w1i, w1o = ffn1_w_in[layer].astype(BF16), ffn1_w_out[layer].astype(BF16)
        w2i, w2o = ffn2_w_in[layer].astype(BF16), ffn2_w_out[layer].astype(BF16)
        wp, wo = w_in[layer].astype(BF16), w_out[layer].astype(BF16)
        row = lambda a: a[layer][None, :]
        g1, b1, g2, b2, g3, b3 = row(ln1_g), row(ln1_b), row(ln2_g), row(ln2_b), row(ln3_g), row(ln3_b)
        vg, vb, ga, gb = row(sgu_v_g), row(sgu_v_b), row(out_a_g), row(out_b_g)
        sgw = sgu_w[layer]
        sg_bias = jnp.repeat(sgu_b[layer].T, GROUP_B, axis=1)
        corner = jnp.where(tril[:n_new, :n_new], sgw[:, :n_new, :n_new], 0.0)
        coef = jnp.repeat(jnp.transpose(corner, (1, 2, 0)).reshape(n_new * n_new, N_GROUPS_B), GROUP_B, axis=1)

        ffn2_consts = (w2i, w2o, g3, b3)
        mix_tail = (vg, vb, ga, gb, wo, g2, b2)

        x1, h = _ffn_proj_block(xp, w1i, w1o, g1, b1, wp, alpha)
        x1s, hs = _ffn_proj_block(xs, w1i, w1o, g1, b1, wp, alpha)
        hs_b = jnp.transpose(hs.reshape(n_new, dec_batch, -1), (1, 0, 2))
        kt = jnp.transpose(cache_k[layer], (0, 2, 3, 1))
        vt = jnp.transpose(cache_v[layer], (0, 2, 3, 1))

        o_a, kt_p, vt_p = _attn_prompt(h, batch, seq)
        xp, o_s, kt_new, vt_new = _prompt_tail(x1, o_a, h, (sgw, sg_bias) + mix_tail, ffn2_consts, hs_b, kt, vt, alpha)
        kp_l.append(jnp.transpose(kt_p, (0, 3, 1, 2)))
        vp_l.append(jnp.transpose(vt_p, (0, 3, 1, 2)))

        o_s = jnp.transpose(o_s, (1, 0, 2)).reshape(n_new * dec_batch, D_A)
        xs, vn_s = _mix_ffn_sample(x1s, o_s, hs, (coef, sg_bias[:n_new]) + mix_tail, ffn2_consts, alpha)
        ks_l.append(jnp.transpose(kt_new, (0, 3, 1, 2)))
        vs_l.append(jnp.transpose(vt_new, (0, 3, 1, 2)))
        us_l.append(jnp.transpose(vn_s.reshape(n_new, dec_batch, D_B), (1, 0, 2)))

    y_p = xp.reshape(batch, seq, D_MODEL)
    y_s = jnp.transpose(xs.reshape(n_new, dec_batch, D_MODEL), (1, 0, 2))
    return (y_p, y_s, jnp.stack(kp_l), jnp.stack(vp_l), jnp.stack(ks_l), jnp.stack(vs_l), jnp.stack(us_l))
```
